```python
import jax, jax.numpy as jnp
from jax import lax
import numpy as np

D_MODEL = 1024
BATCH = 16
SEQ = 2048
DEPTH = 2

GRID_W = 64
CTX_LEN = 256
D_MIX = D_MODEL
RG_WIDTH = 3 * D_MIX // 8
RG_HEADS = 6
RG_HEAD_DIM = RG_WIDTH // RG_HEADS
RG_CONV = 4
RG_C = 8.0
M_WIDTH = 3 * D_MIX // 8
M_HEADS = 4
M_HEAD_DIM = M_WIDTH // M_HEADS
M_CHUNK = 128
CM_WIDTH = D_MIX - RG_WIDTH - M_WIDTH
CM_GROUPS = 4
CM_GROUP_DIM = CM_WIDTH // CM_GROUPS
CM_CHUNK = 128
D_IN = 2 * RG_WIDTH + 4 * M_WIDTH + 4 * M_HEADS + 2 * CM_WIDTH
N_EXPERTS = 256
TOP_K = 8
N_GROUPS = 8
TOPK_GROUPS = 4
MOE_FF = (2048 * D_MODEL // 7168) // 32 * 32
ROUTED_SCALE = 2.5
EXPERT_BLOCK = 128
EPS = 1e-6

kernel_name = "hybrid_rglru_mlstm_gmlp_moe_prefix_dit"

F32 = jnp.float32


def rms_norm(x, g):
    xf = x.astype(F32)
    y = xf * lax.rsqrt(jnp.mean(xf * xf, axis=-1, keepdims=True) + EPS)
    return (y * g.astype(F32)).astype(x.dtype)


def pos_embed_2d(rows, dim):
    quarter = dim // 4
    omega = 1.0 / (10000.0 ** (jnp.arange(quarter, dtype=F32) / quarter))
    row = jnp.repeat(jnp.arange(rows, dtype=F32), GRID_W)
    col = jnp.tile(jnp.arange(GRID_W, dtype=F32), rows)

    def axis_embed(p):
        ang = p[:, None] * omega[None, :]
        return jnp.concatenate([jnp.sin(ang), jnp.cos(ang)], axis=-1)

    return jnp.concatenate([axis_embed(row), axis_embed(col)], axis=-1)


def adaln_params(cond, w_mod, b_mod):
    m = jax.nn.silu(cond) @ w_mod + b_mod
    m = m.reshape(cond.shape[0], 6, cond.shape[-1])
    return jnp.moveaxis(m, 1, 0)[:, :, None, :]


def modulate(z, shift, scale):
    return z * (1.0 + scale) + shift


def split_columns(p):
    sizes = (RG_WIDTH, RG_WIDTH, M_WIDTH, M_WIDTH, M_WIDTH, M_WIDTH, 4 * M_HEADS, 2 * CM_WIDTH)
    return jnp.split(p, np.cumsum(sizes)[:-1].tolist(), axis=-1)


def short_conv(z, w, b):
    C = z.shape[-1]
    y = lax.conv_general_dilated(z, w[:, None, :].astype(z.dtype), window_strides=(1,),
                                 padding=[(2, 1)], dimension_numbers=('NWC', 'WIO', 'NWC'),
                                 feature_group_count=C)
    return y + b


def rglru_gates(z, w, b, lam):
    B, T, W = z.shape
    zb = z.reshape(B, T, RG_HEADS, RG_HEAD_DIM)
    pre = jnp.einsum('bthi,ghij->gbthj', zb, w.astype(F32)).reshape(2, B, T, W) + b.astype(F32)[:, None, None, :]
    r = jax.nn.sigmoid(pre[0])
    i = jax.nn.sigmoid(pre[1])
    log_a = -RG_C * r * jax.nn.softplus(-lam.astype(F32))
    return jnp.exp(log_a), jnp.sqrt(-jnp.expm1(2.0 * log_a)) * (i * z)


def linear_recurrence(a, u, h0):
    def combine(left, right):
        a_l, u_l = left
        a_r, u_r = right
        return a_l * a_r, a_r * u_l + u_r
    a_cum, u_cum = lax.associative_scan(combine, (a, u), axis=1)
    return a_cum * h0[:, None, :] + u_cum


def rglru_bidir(z_lat, z_ctx, gate_w, gate_b, lam, need_ctx):
    z_lat = z_lat.astype(F32)
    z_ctx = z_ctx.astype(F32)
    y_lat, y_ctx = 0.0, 0.0
    for d in range(2):
        flip = (lambda t: jnp.flip(t, axis=1)) if d else (lambda t: t)
        a_c, u_c = rglru_gates(flip(z_ctx), gate_w[d], gate_b[d], lam[d])
        h_c = linear_recurrence(a_c, u_c, jnp.zeros_like(u_c[:, 0]))
        a_x, u_x = rglru_gates(flip(z_lat), gate_w[d], gate_b[d], lam[d])
        h_x = linear_recurrence(a_x, u_x, h_c[:, -1])
        y_lat = y_lat + flip(h_x)
        if need_ctx:
            y_ctx = y_ctx + flip(h_c)
    return y_lat, (y_ctx if need_ctx else None)


def mlstm_prepare(q, k, v, gt, gate_b):
    B, T, _ = q.shape
    heads = lambda t: t.astype(F32).reshape(B, T, M_HEADS, M_HEAD_DIM).transpose(0, 2, 1, 3)
    pre = gt.astype(F32).reshape(B, T, 2, 2, M_HEADS) + gate_b.astype(F32)
    pre = pre.transpose(2, 3, 0, 4, 1)
    return heads(q), heads(k) * (M_HEAD_DIM ** -0.5), heads(v), pre[:, 0], jax.nn.log_sigmoid(pre[:, 1])


def mlstm_chunkwise(q, k, v, li, lf, state0, with_out):
    B, H, T, dk = k.shape
    dv = v.shape[-1]
    L = M_CHUNK
    NC = T // L
    kc = k.reshape(B, H, NC, L, dk)
    vc = v.reshape(B, H, NC, L, dv)
    lic = li.reshape(B, H, NC, L)
    bc = jnp.cumsum(lf.reshape(B, H, NC, L), axis=-1)
    g = bc[..., -1]
    w_end = g[..., None] - bc + lic

    def step(carry, inp):
        C, n, m = carry
        kk, vv, ww, gg = inp
        m_new = jnp.maximum(gg + m, ww.max(-1))
        decay = jnp.exp(gg + m - m_new)
        ws = jnp.exp(ww - m_new[..., None])
        C_new = decay[..., None, None] * C + jnp.einsum('bhl,bhlk,bhlv->bhkv', ws, kk, vv)
        n_new = decay[..., None] * n + jnp.einsum('bhl,bhlk->bhk', ws, kk)
        return (C_new, n_new, m_new), (C, n, m)

    xs = (jnp.moveaxis(kc, 2, 0), jnp.moveaxis(vc, 2, 0), jnp.moveaxis(w_end, 2, 0), jnp.moveaxis(g, 2, 0))
    final, (C_st, n_st, m_st) = lax.scan(step, state0, xs)
    if not with_out:
        return None, final
    C_st = jnp.moveaxis(C_st, 0, 2)
    n_st = jnp.moveaxis(n_st, 0, 2)
    m_st = jnp.moveaxis(m_st, 0, 2)
    qc = q.reshape(B, H, NC, L, dk)
    mask = jnp.tril(jnp.ones((L, L), dtype=bool))
    D = jnp.where(mask, bc[..., :, None] - bc[..., None, :] + lic[..., None, :], -jnp.inf)
    inter = bc + m_st[..., None]
    m = jnp.maximum(D.max(-1), inter)
    S = jnp.einsum('bhntk,bhnsk->bhnts', qc, kc) * jnp.exp(D - m[..., None])
    e_inter = jnp.exp(inter - m)
    num = jnp.einsum('bhnts,bhnsv->bhntv', S, vc) + e_inter[..., None] * jnp.einsum('bhntk,bhnkv->bhntv', qc, C_st)
    den = S.sum(-1) + e_inter * jnp.einsum('bhntk,bhnk->bhnt', qc, n_st)
    h = num / jnp.maximum(jnp.abs(den), jnp.exp(-m))[..., None]
    return h.reshape(B, H, T, dv), final


def mlstm_readout(hsum, o, norm_g):
    B, H, T, dv = hsum.shape
    hs = hsum.transpose(0, 2, 1, 3)
    hs = hs * lax.rsqrt(jnp.mean(hs * hs, axis=-1, keepdims=True) + EPS)
    hs = hs.reshape(B, T, H * dv) * norm_g.astype(F32)
    return (jax.nn.sigmoid(o.astype(F32)) * hs).astype(o.dtype)


def mlstm_bidir(lat_cols, ctx_cols, gate_b, norm_g, need_ctx):
    q, k, v, li, lf = mlstm_prepare(lat_cols[0], lat_cols[1], lat_cols[2], lat_cols[4], gate_b)
    qc, kc, vc, lic, lfc = mlstm_prepare(ctx_cols[0], ctx_cols[1], ctx_cols[2], ctx_cols[4], gate_b)
    B = q.shape[0]
    zero = (jnp.zeros((B, M_HEADS, M_HEAD_DIM, M_HEAD_DIM), F32),
            jnp.zeros((B, M_HEADS, M_HEAD_DIM), F32),
            jnp.zeros((B, M_HEADS), F32))
    lat_sum, ctx_sum = 0.0, 0.0
    for d in range(2):
        fl = (lambda t: jnp.flip(t, axis=2)) if d else (lambda t: t)
        h_c, state = mlstm_chunkwise(fl(qc), fl(kc), fl(vc), fl(lic[d]), fl(lfc[d]), zero, need_ctx)
        h_x, _ = mlstm_chunkwise(fl(q), fl(k), fl(v), fl(li[d]), fl(lf[d]), state, True)
        lat_sum = lat_sum + fl(h_x)
        if need_ctx:
            ctx_sum = ctx_sum + fl(h_c)
    y_lat = mlstm_readout(lat_sum, lat_cols[3], norm_g)
    y_ctx = mlstm_readout(ctx_sum, ctx_cols[3], norm_g) if need_ctx else None
    return y_lat, y_ctx


def chunk_mlp(p, norm_g, w_s, b_s):
    u, v = jnp.split(p, 2, axis=-1)
    v = rms_norm(v, norm_g)
    B, T, _ = v.shape
    vb = v.reshape(B, T // CM_CHUNK, CM_CHUNK, CM_GROUPS, CM_GROUP_DIM)
    mixed = jnp.einsum('gts,bnsgc->bntgc', w_s, vb) + b_s.T[None, None, :, :, None]
    return (u * mixed.reshape(B, T, CM_WIDTH)).astype(p.dtype)


def swiglu(x, w_g, w_u, w_d):
    return (jax.nn.silu(x @ w_g) * (x @ w_u)) @ w_d


def routed_experts(xt, eidx, w, e_gate, e_up, e_down):
    T, D = xt.shape
    K = eidx.shape[1]
    E = e_gate.shape[0]
    A = T * K
    flat_e = eidx.reshape(A)
    order = jnp.argsort(flat_e)
    e_sorted = flat_e[order]
    tok_sorted = order // K
    w_sorted = w.reshape(A)[order]
    counts = jnp.bincount(flat_e, length=E)
    padded = (counts + EXPERT_BLOCK - 1) // EXPERT_BLOCK * EXPERT_BLOCK
    pad_end = jnp.cumsum(padded)
    pad_start = pad_end - padded
    start = jnp.cumsum(counts) - counts
    dest = pad_start[e_sorted] + jnp.arange(A) - start[e_sorted]
    n_blocks = -(-(A + E * (EXPERT_BLOCK - 1)) // EXPERT_BLOCK)
    n_rows = n_blocks * EXPERT_BLOCK
    row_tok = jnp.zeros((n_rows,), jnp.int32).at[dest].set(tok_sorted)
    row_w = jnp.zeros((n_rows,), w.dtype).at[dest].set(w_sorted)
    block_e = jnp.minimum(jnp.searchsorted(pad_end, jnp.arange(n_blocks) * EXPERT_BLOCK, side='right'), E - 1)

    def block_ffn(args):
        tok_b, w_b, e = args
        xb = xt[tok_b]
        hb = jax.nn.silu(xb @ e_gate[e]) * (xb @ e_up[e])
        return ((hb @ e_down[e]) * w_b[:, None]).astype(xt.dtype)

    yb = lax.map(block_ffn, (row_tok.reshape(n_blocks, EXPERT_BLOCK),
                             row_w.reshape(n_blocks, EXPERT_BLOCK), block_e))
    return jnp.zeros_like(xt).at[row_tok].add(yb.reshape(n_rows, D))


def moe(xt, router_w, router_b, e_gate, e_up, e_down, s_gate, s_up, s_down):
    T = xt.shape[0]
    scores = jax.nn.sigmoid((xt @ router_w).astype(F32))
    sel = scores + router_b.astype(F32)
    per_group = N_EXPERTS // N_GROUPS
    group_score = lax.top_k(sel.reshape(T, N_GROUPS, per_group), 2)[0].sum(-1)
    _, gidx = lax.top_k(group_score, TOPK_GROUPS)
    gmask = (gidx[:, :, None] == jnp.arange(N_GROUPS)[None, None, :]).any(axis=1)
    sel = jnp.where(jnp.repeat(gmask, per_group, axis=1), sel, -jnp.inf)
    _, eidx = lax.top_k(sel, TOP_K)
    w = jnp.take_along_axis(scores, eidx, axis=1)
    w = w / w.sum(-1, keepdims=True) * ROUTED_SCALE
    return routed_experts(xt, eidx, w, e_gate, e_up, e_down) + swiglu(xt, s_gate, s_up, s_down)


def hybrid_layer(x, h, c, c_ctx, norm1_g, norm2_g, w_mod, b_mod, w_in, rg_conv_w, rg_conv_b,
                 rg_gate_w, rg_gate_b, rg_lambda, mlstm_gate_b, mlstm_norm_g, cm_norm_g, cm_w, cm_b,
                 w_out, router_w, router_b, exp_gate, exp_up, exp_down, sh_gate, sh_up, sh_down, last):
    need_ctx = not last
    mod_x = adaln_params(c, w_mod, b_mod)
    mod_h = adaln_params(c_ctx[None], w_mod, b_mod)
    xa = modulate(rms_norm(x, norm1_g), mod_x[0], mod_x[1])
    ha = modulate(rms_norm(h, norm1_g), mod_h[0], mod_h[1])
    rx, rgx, qx, kx, vx, ox, gx, cmx = split_columns(xa @ w_in)
    rh, rgh, qh, kh, vh, oh, gh, cmh = split_columns(ha @ w_in)

    r_lat, r_ctx = rglru_bidir(short_conv(rx, rg_conv_w, rg_conv_b), short_conv(rh, rg_conv_w, rg_conv_b),
                               rg_gate_w, rg_gate_b, rg_lambda, need_ctx)
    m_lat, m_ctx = mlstm_bidir((qx, kx, vx, ox, gx), (qh, kh, vh, oh, gh), mlstm_gate_b, mlstm_norm_g, need_ctx)
    mix_lat = jnp.concatenate([(jax.nn.gelu(rgx) * r_lat).astype(x.dtype), m_lat,
                               chunk_mlp(jax.nn.gelu(cmx), cm_norm_g, cm_w, cm_b)], axis=-1)
    x = x + mod_x[2] * (mix_lat @ w_out)
    if need_ctx:
        mix_ctx = jnp.concatenate([(jax.nn.gelu(rgh) * r_ctx).astype(h.dtype), m_ctx,
                                   chunk_mlp(jax.nn.gelu(cmh), cm_norm_g, cm_w, cm_b)], axis=-1)
        h = h + mod_h[2] * (mix_ctx @ w_out)

    moe_w = (router_w, router_b, exp_gate, exp_up, exp_down, sh_gate, sh_up, sh_down)
    xb = modulate(rms_norm(x, norm2_g), mod_x[3], mod_x[4])
    D = x.shape[-1]
    if last:
        x = x + mod_x[5] * moe(xb.reshape(-1, D), *moe_w).reshape(x.shape)
        return x, None
    hb = modulate(rms_norm(h, norm2_g), mod_h[3], mod_h[4])
    nx = x.shape[0] * x.shape[1]
    y = moe(jnp.concatenate([xb.reshape(-1, D), hb.reshape(-1, D)], axis=0), *moe_w)
    x = x + mod_x[5] * y[:nx].reshape(x.shape)
    h = h + mod_h[5] * y[nx:].reshape(h.shape)
    return x, h


def setup_inputs(seed: int = 0) -> dict:
    key = jax.random.key(seed)
    ks_arr = jax.random.split(key, 32)
    ks = iter([ks_arr[i] for i in range(32)])

    def normal(shape, std):
        return jax.random.normal(next(ks), shape, F32) * std

    def uniform(shape, lo, hi):
        return jax.random.uniform(next(ks), shape, F32, lo, hi)

    L = DEPTH
    x = normal((BATCH, SEQ, D_MODEL), 1.0)
    c = normal((BATCH, D_MODEL), 1.0)
    ctx = normal((BATCH, CTX_LEN, D_MODEL), 1.0)
    c_ctx = normal((D_MODEL,), 1.0)
    norm1_g = 1.0 + normal((L, D_MODEL), 0.02)
    norm2_g = 1.0 + normal((L, D_MODEL), 0.02)
    w_mod = normal((L, D_MODEL, 6 * D_MODEL), 0.5 * D_MODEL ** -0.5)
    b_mod = normal((L, 6 * D_MODEL), 0.02)
    w_in = normal((L, D_MODEL, D_IN), D_MODEL ** -0.5)
    rg_conv_w = normal((L, RG_CONV, RG_WIDTH), RG_CONV ** -0.5)
    rg_conv_b = normal((L, RG_WIDTH), 0.02)
    rg_gate_w = normal((L, 2, 2, RG_HEADS, RG_HEAD_DIM, RG_HEAD_DIM), RG_HEAD_DIM ** -0.5)
    rg_gate_b = normal((L, 2, 2, RG_WIDTH), 0.02)
    a0 = uniform((L, 2, RG_WIDTH), 0.9, 0.999)
    s = a0 ** (1.0 / RG_C)
    rg_lambda = jnp.log(s) - jnp.log1p(-s)
    mlstm_gate_b = jnp.stack([normal((L, 2, M_HEADS), 0.1), uniform((L, 2, M_HEADS), 3.0, 6.0)], axis=2)
    mlstm_norm_g = 1.0 + normal((L, M_WIDTH), 0.02)
    cm_norm_g = 1.0 + normal((L, CM_WIDTH), 0.02)
    cm_w = normal((L, CM_GROUPS, CM_CHUNK, CM_CHUNK), CM_CHUNK ** -0.5)
    cm_b = 1.0 + normal((L, CM_GROUPS, CM_CHUNK), 0.02)
    w_out = normal((L, D_MIX, D_MODEL), D_MIX ** -0.5)
    router_w = normal((L, D_MODEL, N_EXPERTS), D_MODEL ** -0.5)
    router_b = normal((L, N_EXPERTS), 0.01)
    exp_gate = normal((L, N_EXPERTS, D_MODEL, MOE_FF), D_MODEL ** -0.5)
    exp_up = normal((L, N_EXPERTS, D_MODEL, MOE_FF), D_MODEL ** -0.5)
    exp_down = normal((L, N_EXPERTS, MOE_FF, D_MODEL), MOE_FF ** -0.5)
    sh_gate = normal((L, D_MODEL, MOE_FF), D_MODEL ** -0.5)
    sh_up = normal((L, D_MODEL, MOE_FF), D_MODEL ** -0.5)
    sh_down = normal((L, MOE_FF, D_MODEL), MOE_FF ** -0.5)
    final_norm_g = 1.0 + normal((D_MODEL,), 0.02)
    return {"x": x, "c": c, "ctx": ctx, "c_ctx": c_ctx, "norm1_g": norm1_g, "norm2_g": norm2_g,
            "w_mod": w_mod, "b_mod": b_mod, "w_in": w_in, "rg_conv_w": rg_conv_w, "rg_conv_b": rg_conv_b,
            "rg_gate_w": rg_gate_w, "rg_gate_b": rg_gate_b, "rg_lambda": rg_lambda,
            "mlstm_gate_b": mlstm_gate_b, "mlstm_norm_g": mlstm_norm_g, "cm_norm_g": cm_norm_g,
            "cm_w": cm_w, "cm_b": cm_b, "w_out": w_out, "router_w": router_w, "router_b": router_b,
            "exp_gate": exp_gate, "exp_up": exp_up, "exp_down": exp_down, "sh_gate": sh_gate,
            "sh_up": sh_up, "sh_down": sh_down, "final_norm_g": final_norm_g}


def reference(x, c, ctx, c_ctx, norm1_g, norm2_g, w_mod, b_mod, w_in, rg_conv_w, rg_conv_b,
              rg_gate_w, rg_gate_b, rg_lambda, mlstm_gate_b, mlstm_norm_g, cm_norm_g, cm_w, cm_b,
              w_out, router_w, router_b, exp_gate, exp_up, exp_down, sh_gate, sh_up, sh_down,
              final_norm_g):
    rows = x.shape[1] // GRID_W
    x = x + pos_embed_2d(rows, x.shape[-1]).astype(x.dtype)[None]
    h = ctx
    for l in range(DEPTH):
        x, h = hybrid_layer(x, h, c, c_ctx, norm1_g[l], norm2_g[l], w_mod[l], b_mod[l], w_in[l],
                            rg_conv_w[l], rg_conv_b[l], rg_gate_w[l], rg_gate_b[l], rg_lambda[l],
                            mlstm_gate_b[l], mlstm_norm_g[l], cm_norm_g[l], cm_w[l], cm_b[l], w_out[l],
                            router_w[l], router_b[l], exp_gate[l], exp_up[l], exp_down[l],
                            sh_gate[l], sh_up[l], sh_down[l], l == DEPTH - 1)
    return rms_norm(x, final_norm_g)
```

```python
import functools

import numpy as np
import jax
import jax.numpy as jnp
from jax import lax
from jax.experimental import pallas as pl
from jax.experimental.pallas import tpu as pltpu

F32 = jnp.float32
MXU_DTYPE = jnp.bfloat16
LANE = 128
SUBLANE = 8
VMEM_LIMIT = 56 * 1024 * 1024

GRID_W = 64
RG_HEADS = 6
RG_C = 8.0
M_HEADS = 4
M_CHUNK = 128
CM_GROUPS = 4
CM_CHUNK = 128
N_GROUPS = 8
TOPK_GROUPS = 4
TOP_K = 8
ROUTED_SCALE = 2.5
EPS = 1e-6

TOKEN_TILE = 256
MOE_BLOCK = 256


def _cparams(*sem):
    return pltpu.CompilerParams(dimension_semantics=sem, vmem_limit_bytes=VMEM_LIMIT)


def _dot(a, b):
    return jnp.dot(a.astype(MXU_DTYPE), b.astype(MXU_DTYPE), preferred_element_type=F32)


def _sigmoid(x):
    return 1.0 / (1.0 + jnp.exp(-x))


def _gelu_tanh(x):
    return 0.5 * x * (1.0 + jnp.tanh(0.7978845608028654 * (x + 0.044715 * (x * x * x))))


def _softplus(x):
    return jnp.maximum(x, 0.0) + jnp.log(1.0 + jnp.exp(-jnp.abs(x)))


def _rms(x, g):
    return x * lax.rsqrt(jnp.mean(x * x, axis=-1, keepdims=True) + EPS) * g


def _adaln_kernel(c_ref, w_ref, b_ref, o_ref):
    c = c_ref[...]
    o_ref[...] = _dot(c * _sigmoid(c), w_ref[...]) + b_ref[...]


def _adaln(cond, w_mod, b_mod):
    n, d = cond.shape
    n_out = w_mod.shape[1]
    tn = 1536
    return pl.pallas_call(
        _adaln_kernel,
        grid=(n_out // tn,),
        in_specs=[pl.BlockSpec((n, d), lambda j: (0, 0)),
                  pl.BlockSpec((d, tn), lambda j: (0, j)),
                  pl.BlockSpec((1, tn), lambda j: (0, j))],
        out_specs=pl.BlockSpec((n, tn), lambda j: (0, j)),
        out_shape=jax.ShapeDtypeStruct((n, n_out), F32),
        compiler_params=_cparams("arbitrary"),
        name="adaln",
    )(cond, w_mod, b_mod.reshape(1, n_out))


def _inproj_kernel(x_ref, mod_ref, g_ref, w_ref, rg_ref, qkvo_ref, gt_ref, cm_ref, *, splits):
    x = x_ref[0]
    shift = mod_ref[0, 0, 0:1, :]
    scale = mod_ref[0, 0, 1:2, :]
    xa = (_rms(x, g_ref[...]) * (1.0 + scale) + shift).astype(MXU_DTYPE)
    a = 0
    for ref, width in zip((rg_ref, qkvo_ref, gt_ref, cm_ref), splits):
        ref[0] = jnp.dot(xa, w_ref[:, a:a + width], preferred_element_type=F32)
        a += width


def _inproj(xall, mod, g, w_p, splits, ctx_tiles):
    b, s, d = xall.shape
    tm = TOKEN_TILE
    n_p = w_p.shape[1]
    tok = lambda bi, i: (bi, i, 0)
    return pl.pallas_call(
        functools.partial(_inproj_kernel, splits=splits),
        grid=(b, s // tm),
        in_specs=[pl.BlockSpec((1, tm, d), tok),
                  pl.BlockSpec((1, 1, 8, d), lambda bi, i: (bi, jnp.minimum(i // ctx_tiles, 1), 0, 0)),
                  pl.BlockSpec((1, d), lambda bi, i: (0, 0)),
                  pl.BlockSpec((d, n_p), lambda bi, i: (0, 0))],
        out_specs=[pl.BlockSpec((1, tm, w), tok) for w in splits],
        out_shape=[jax.ShapeDtypeStruct((b, s, w), F32) for w in splits],
        compiler_params=_cparams("parallel", "arbitrary"),
        name="inproj",
    )(xall, mod, g, w_p)


def _rglru_kernel(rx_ref, rgx_ref, cw_ref, cb_ref, wg_ref, bg_ref, lam_ref, y_ref,
                  af, uf, ab, ub, *, ctx_len):
    s = rx_ref.shape[1]
    x = rx_ref[0]
    t = lax.broadcasted_iota(jnp.int32, (s, LANE), 0)
    is_lat = t >= ctx_len
    pos = jnp.where(is_lat, t - ctx_len, t)
    seg_len = jnp.where(is_lat, s - ctx_len, ctx_len)
    xm2 = jnp.where(pos >= 2, pltpu.roll(x, 2, 0), 0.0)
    xm1 = jnp.where(pos >= 1, pltpu.roll(x, 1, 0), 0.0)
    xp1 = jnp.where(pos < seg_len - 1, pltpu.roll(x, s - 1, 0), 0.0)
    cw = cw_ref[...]
    z = cw[0:1] * xm2 + cw[1:2] * xm1 + cw[2:3] * x + cw[3:4] * xp1 + cb_ref[...]
    pre = _dot(z, wg_ref[0]) + bg_ref[0]
    sp = _softplus(-lam_ref[...])
    for d, (a_ref, u_ref) in enumerate(((af, uf), (ab, ub))):
        r = _sigmoid(pre[:, (2 * d) * LANE:(2 * d + 1) * LANE])
        i = _sigmoid(pre[:, (2 * d + 1) * LANE:(2 * d + 2) * LANE])
        log_a = -RG_C * r * sp[d:d + 1]
        a_ref[...] = jnp.exp(log_a)
        u_ref[...] = jnp.sqrt(1.0 - jnp.exp(2.0 * log_a)) * (i * z)

    row = lax.broadcasted_iota(jnp.int32, (SUBLANE, LANE), 0)
    n_tiles = s // SUBLANE
    n_ctx = ctx_len // SUBLANE

    def body(n, carry):
        hf_prev, hb_prev = carry
        of = pl.multiple_of(n * SUBLANE, SUBLANE)
        a = af[pl.ds(of, SUBLANE), :]
        u = uf[pl.ds(of, SUBLANE), :]
        for sh in (1, 2, 4):
            a_sh = jnp.where(row >= sh, pltpu.roll(a, sh, 0), 1.0)
            u_sh = jnp.where(row >= sh, pltpu.roll(u, sh, 0), 0.0)
            u = a * u_sh + u
            a = a * a_sh
        h = a * hf_prev + u
        uf[pl.ds(of, SUBLANE), :] = h
        hf_new = h[SUBLANE - 1:SUBLANE, :]
        tb = jnp.where(n < n_ctx, n_ctx - 1 - n, n_tiles - 1 + n_ctx - n)
        ob = pl.multiple_of(tb * SUBLANE, SUBLANE)
        a = ab[pl.ds(ob, SUBLANE), :]
        u = ub[pl.ds(ob, SUBLANE), :]
        for sh in (1, 2, 4):
            a_sh = jnp.where(row < SUBLANE - sh, pltpu.roll(a, SUBLANE - sh, 0), 1.0)
            u_sh = jnp.where(row < SUBLANE - sh, pltpu.roll(u, SUBLANE - sh, 0), 0.0)
            u = a * u_sh + u
            a = a * a_sh
        h = a * hb_prev + u
        ub[pl.ds(ob, SUBLANE), :] = h
        return hf_new, h[0:1, :]

    zero = jnp.zeros((1, LANE), F32)
    lax.fori_loop(0, n_tiles, body, (zero, zero))
    y_ref[0] = (uf[...] + ub[...]) * _gelu_tanh(rgx_ref[0])


def _rglru(rg, conv_w, conv_b, wg, bg, lam, ctx_len):
    b, s, w2 = rg.shape
    nt = w2 // 2 // LANE
    return pl.pallas_call(
        functools.partial(_rglru_kernel, ctx_len=ctx_len),
        grid=(b, nt),
        in_specs=[pl.BlockSpec((1, s, LANE), lambda bi, j: (bi, 0, j)),
                  pl.BlockSpec((1, s, LANE), lambda bi, j: (bi, 0, nt + j)),
                  pl.BlockSpec((conv_w.shape[0], LANE), lambda bi, j: (0, j)),
                  pl.BlockSpec((1, LANE), lambda bi, j: (0, j)),
                  pl.BlockSpec((1, LANE, 4 * LANE), lambda bi, j: (j, 0, 0)),
                  pl.BlockSpec((1, 1, 4 * LANE), lambda bi, j: (j, 0, 0)),
                  pl.BlockSpec((2, LANE), lambda bi, j: (0, j))],
        out_specs=pl.BlockSpec((1, s, LANE), lambda bi, j: (bi, 0, j)),
        out_shape=jax.ShapeDtypeStruct((b, s, nt * LANE), F32),
        scratch_shapes=[pltpu.VMEM((s, LANE), F32)] * 4,
        compiler_params=_cparams("parallel", "arbitrary"),
        name="rglru",
    )(rg, rg, conv_w, conv_b, wg, bg, lam)


def _mlstm_kernel(q_ref, k_ref, v_ref, o_ref, g_ref, gb_ref, ng_ref, out_ref,
                  gs, hf, hb, *, ctx_len, head_dim):
    s = q_ref.shape[1]
    ln = M_CHUNK
    n_chunks = s // ln
    n_ctx = ctx_len // ln
    head = pl.program_id(1)
    lane = lax.broadcasted_iota(jnp.int32, (s, LANE), 1)
    tmod = lax.broadcasted_iota(jnp.int32, (s, LANE), 0) % ln

    gates = g_ref[0] + gb_ref[...]

    def col(c):
        return jnp.sum(jnp.where(lane == c, gates, 0.0), axis=1, keepdims=True)

    li_f = col(head)
    lf_f = -_softplus(-col(M_HEADS + head))
    li_b = col(2 * M_HEADS + head)
    lf_b = -_softplus(-col(3 * M_HEADS + head))
    acc = jnp.where(lane == 0, lf_f, 0.0) + jnp.where(lane == 2, lf_b, 0.0)
    sh = 1
    while sh < ln:
        down = jnp.where(tmod >= sh, pltpu.roll(acc, sh, 0), 0.0)
        up = jnp.where(tmod < ln - sh, pltpu.roll(acc, s - sh, 0), 0.0)
        acc = acc + jnp.where(lane == 0, down, up)
        sh *= 2
    gs[...] = acc + jnp.where(lane == 1, li_f, 0.0) + jnp.where(lane == 3, li_b, 0.0)

    tt = lax.broadcasted_iota(jnp.int32, (ln, ln), 0)
    ss = lax.broadcasted_iota(jnp.int32, (ln, ln), 1)
    lane_c = lax.broadcasted_iota(jnp.int32, (ln, LANE), 1)
    k_scale = head_dim ** -0.5

    def chunk_step(c, c_state, m_st, rev):
        o = pl.multiple_of(c * ln, ln)
        blk = gs[pl.ds(o, ln), :]
        bc = blk[:, 2 * rev:2 * rev + 1]
        li = blk[:, 2 * rev + 1:2 * rev + 2]
        w = li - bc
        w_row = jnp.transpose(jnp.broadcast_to(w, (ln, ln)))
        mask = (ss >= tt) if rev else (ss <= tt)
        dmat = jnp.where(mask, bc + w_row, -jnp.inf)
        inter = bc + m_st
        m = jnp.maximum(jnp.max(dmat, axis=1, keepdims=True), inter)
        p = jnp.exp(dmat - m)
        q = q_ref[0, pl.ds(o, ln), :].astype(MXU_DTYPE)
        kf = k_ref[0, pl.ds(o, ln), :] * k_scale
        v = jnp.where(lane_c == head_dim, 1.0, v_ref[0, pl.ds(o, ln), :]).astype(MXU_DTYPE)
        sc = lax.dot_general(q, kf.astype(MXU_DTYPE), (((1,), (1,)), ((), ())),
                             preferred_element_type=F32) * p
        e_inter = jnp.exp(inter - m)
        num = (jnp.dot(sc.astype(MXU_DTYPE), v, preferred_element_type=F32)
               + e_inter * jnp.dot(q, c_state.astype(MXU_DTYPE), preferred_element_type=F32))
        den = jnp.sum(jnp.where(lane_c == head_dim, num, 0.0), axis=1, keepdims=True)
        h = num / jnp.maximum(jnp.abs(den), jnp.exp(-m))
        g = bc[0:1, :] if rev else bc[ln - 1:ln, :]
        w_end = g + w
        m_new = jnp.maximum(g + m_st, jnp.max(w_end, axis=0, keepdims=True))
        decay = jnp.exp(g + m_st - m_new)
        ks = (kf * jnp.exp(w_end - m_new)).astype(MXU_DTYPE)
        c_new = decay * c_state + lax.dot_general(ks, v, (((0,), (0,)), ((), ())),
                                                  preferred_element_type=F32)
        return h, c_new, m_new

    def body(n, carry):
        cf, mf, cb, mb = carry
        h, cf, mf = chunk_step(n, cf, mf, 0)
        hf[pl.ds(pl.multiple_of(n * ln, ln), ln), :] = h
        nb = jnp.where(n < n_ctx, n_ctx - 1 - n, n_chunks - 1 + n_ctx - n)
        h, cb, mb = chunk_step(nb, cb, mb, 1)
        hb[pl.ds(pl.multiple_of(nb * ln, ln), ln), :] = h
        return cf, mf, cb, mb

    c0 = jnp.zeros((LANE, LANE), F32)
    m0 = jnp.zeros((1, 1), F32)
    lax.fori_loop(0, n_chunks, body, (c0, m0, c0, m0))

    hsum = jnp.where(lane < head_dim, hf[...] + hb[...], 0.0)
    ms = jnp.sum(hsum * hsum, axis=1, keepdims=True) * (1.0 / head_dim)
    out_ref[0] = _sigmoid(o_ref[0]) * (hsum * lax.rsqrt(ms + EPS) * ng_ref[0])


def _mlstm(qkvo, gates, gate_b, norm_g, ctx_len, head_dim):
    b, s, _ = qkvo.shape
    h = M_HEADS
    blk = lambda off: pl.BlockSpec((1, s, LANE), lambda bi, hi: (bi, 0, off * h + hi))
    return pl.pallas_call(
        functools.partial(_mlstm_kernel, ctx_len=ctx_len, head_dim=head_dim),
        grid=(b, h),
        in_specs=[blk(0), blk(1), blk(2), blk(3),
                  pl.BlockSpec((1, s, LANE), lambda bi, hi: (bi, 0, 0)),
                  pl.BlockSpec((1, LANE), lambda bi, hi: (0, 0)),
                  pl.BlockSpec((1, 1, LANE), lambda bi, hi: (hi, 0, 0))],
        out_specs=pl.BlockSpec((1, s, LANE), lambda bi, hi: (bi, 0, hi)),
        out_shape=jax.ShapeDtypeStruct((b, s, h * LANE), F32),
        scratch_shapes=[pltpu.VMEM((s, LANE), F32)] * 3,
        compiler_params=_cparams("parallel", "arbitrary"),
        name="mlstm",
    )(qkvo, qkvo, qkvo, qkvo, gates, gate_b, norm_g)


def _gmlp_kernel(cm_ref, ng_ref, ws_ref, bs_ref, y_ref, *, width):
    tm = cm_ref.shape[1]
    gd = width // CM_GROUPS
    lane = lax.broadcasted_iota(jnp.int32, (CM_CHUNK, width), 1)
    for c in range(tm // CM_CHUNK):
        rows = slice(c * CM_CHUNK, (c + 1) * CM_CHUNK)
        u = _gelu_tanh(cm_ref[0, rows, 0:width])
        v = _rms(_gelu_tanh(cm_ref[0, rows, width:2 * width]), ng_ref[...]).astype(MXU_DTYPE)
        mixed = bs_ref[...]
        for g in range(CM_GROUPS):
            mg = jnp.dot(ws_ref[g], v, preferred_element_type=F32)
            mixed = mixed + jnp.where((lane >= g * gd) & (lane < (g + 1) * gd), mg, 0.0)
        y_ref[0, rows, :] = u * mixed


def _gmlp(cm, norm_g, w_s, b_map):
    b, s, w2 = cm.shape
    width = w2 // 2
    tm = TOKEN_TILE
    return pl.pallas_call(
        functools.partial(_gmlp_kernel, width=width),
        grid=(b, s // tm),
        in_specs=[pl.BlockSpec((1, tm, w2), lambda bi, i: (bi, i, 0)),
                  pl.BlockSpec((1, width), lambda bi, i: (0, 0)),
                  pl.BlockSpec(w_s.shape, lambda bi, i: (0, 0, 0)),
                  pl.BlockSpec(b_map.shape, lambda bi, i: (0, 0))],
        out_specs=pl.BlockSpec((1, tm, width), lambda bi, i: (bi, i, 0)),
        out_shape=jax.ShapeDtypeStruct((b, s, width), F32),
        compiler_params=_cparams("parallel", "arbitrary"),
        name="gmlp",
    )(cm, norm_g, w_s, b_map)


def _outproj_router_kernel(yr_ref, ym_ref, yc_ref, x_ref, mod_ref, g_ref, wr_ref, wm_ref, wc_ref,
                           rw_ref, rb_ref, x1_ref, xb_ref, idx_ref, wt_ref):
    mix = (_dot(yr_ref[0], wr_ref[...]) + _dot(ym_ref[0], wm_ref[...]) + _dot(yc_ref[0], wc_ref[...]))
    x1 = x_ref[0] + mod_ref[0, 0, 2:3, :] * mix
    x1_ref[0] = x1
    xb = _rms(x1, g_ref[...]) * (1.0 + mod_ref[0, 0, 4:5, :]) + mod_ref[0, 0, 3:4, :]
    xb_ref[0] = xb.astype(xb_ref.dtype)

    logits = jnp.dot(xb, rw_ref[...], preferred_element_type=F32, precision=lax.Precision.HIGHEST)
    scores = _sigmoid(logits)
    sel = scores + rb_ref[...]
    tm, n_exp = sel.shape
    per_group = n_exp // N_GROUPS
    neg = -jnp.inf
    lane_h = lax.broadcasted_iota(jnp.int32, (tm, LANE), 1)
    lane_hf = lane_h.astype(F32)
    lane_e = lax.broadcasted_iota(jnp.int32, (tm, n_exp), 1)
    lane_ef = lane_e.astype(F32)
    groups_per_tile = LANE // per_group

    gscore = jnp.full((tm, LANE), neg, F32)
    for g in range(N_GROUPS):
        tile = sel[:, (g // groups_per_tile) * LANE:(g // groups_per_tile + 1) * LANE]
        mg = jnp.where(lane_h // per_group == g % groups_per_tile, tile, neg)
        m1 = jnp.max(mg, axis=1, keepdims=True)
        i1 = jnp.min(jnp.where(mg == m1, lane_hf, float(LANE)), axis=1, keepdims=True)
        m2 = jnp.max(jnp.where(lane_hf == i1, neg, mg), axis=1, keepdims=True)
        gscore = jnp.where(lane_h == g, m1 + m2, gscore)
    group_of = (lane_e // per_group).astype(F32)
    allowed = jnp.zeros((tm, n_exp), jnp.bool_)
    for _ in range(TOPK_GROUPS):
        m = jnp.max(gscore, axis=1, keepdims=True)
        gi = jnp.min(jnp.where(gscore == m, lane_hf, float(LANE)), axis=1, keepdims=True)
        gscore = jnp.where(lane_hf == gi, neg, gscore)
        allowed = allowed | (group_of == gi)
    cand = jnp.where(allowed, sel, neg)
    idx_out = jnp.zeros((tm, LANE), F32)
    w_out = jnp.zeros((tm, LANE), F32)
    for k in range(TOP_K):
        m = jnp.max(cand, axis=1, keepdims=True)
        ik = jnp.min(jnp.where(cand == m, lane_ef, float(n_exp)), axis=1, keepdims=True)
        hit = lane_ef == ik
        wk = jnp.sum(jnp.where(hit, scores, 0.0), axis=1, keepdims=True)
        cand = jnp.where(hit, neg, cand)
        idx_out = jnp.where(lane_h == k, ik, idx_out)
        w_out = jnp.where(lane_h == k, wk, w_out)
    w_sum = jnp.sum(w_out, axis=1, keepdims=True)
    idx_ref[0] = idx_out.astype(jnp.int32)
    wt_ref[0] = w_out / w_sum * ROUTED_SCALE


def _outproj_router(y_rg, y_m, y_cm, xall, mod, g2, w_r, w_m, w_c, router_w, router_b, ctx_tiles):
    b, s, d = xall.shape
    tm = TOKEN_TILE
    tok = lambda bi, i: (bi, i, 0)
    full2 = lambda bi, i: (0, 0)
    return pl.pallas_call(
        _outproj_router_kernel,
        grid=(b, s // tm),
        in_specs=[pl.BlockSpec((1, tm, y_rg.shape[2]), tok),
                  pl.BlockSpec((1, tm, y_m.shape[2]), tok),
                  pl.BlockSpec((1, tm, y_cm.shape[2]), tok),
                  pl.BlockSpec((1, tm, d), tok),
                  pl.BlockSpec((1, 1, 8, d), lambda bi, i: (bi, jnp.minimum(i // ctx_tiles, 1), 0, 0)),
                  pl.BlockSpec((1, d), full2),
                  pl.BlockSpec(w_r.shape, full2),
                  pl.BlockSpec(w_m.shape, full2),
                  pl.BlockSpec(w_c.shape, full2),
                  pl.BlockSpec(router_w.shape, full2),
                  pl.BlockSpec((1, router_w.shape[1]), full2)],
        out_specs=[pl.BlockSpec((1, tm, d), tok),
                   pl.BlockSpec((1, tm, d), tok),
                   pl.BlockSpec((1, tm, LANE), tok),
                   pl.BlockSpec((1, tm, LANE), tok)],
        out_shape=[jax.ShapeDtypeStruct((b, s, d), F32),
                   jax.ShapeDtypeStruct((b, s, d), MXU_DTYPE),
                   jax.ShapeDtypeStruct((b, s, LANE), jnp.int32),
                   jax.ShapeDtypeStruct((b, s, LANE), F32)],
        compiler_params=_cparams("parallel", "arbitrary"),
        name="outproj_router",
    )(y_rg, y_m, y_cm, xall, mod, g2, w_r, w_m, w_c, router_w, router_b)


def _moe_ffn_kernel(be_ref, nu_ref, xs_ref, rw_ref, wg_ref, wu_ref, wd_ref, out_ref, wgb, wub, wdb):
    i = pl.program_id(0)
    first = (i == 0) | (be_ref[i] != be_ref[jnp.maximum(i - 1, 0)])

    @pl.when(first)
    def _():
        wgb[...] = wg_ref[0].astype(MXU_DTYPE)
        wub[...] = wu_ref[0].astype(MXU_DTYPE)
        wdb[...] = wd_ref[0].astype(MXU_DTYPE)

    @pl.when(i < nu_ref[0])
    def _():
        x = xs_ref[...]
        hg = jnp.dot(x, wgb[...], preferred_element_type=F32)
        hu = jnp.dot(x, wub[...], preferred_element_type=F32)
        h = (hg * _sigmoid(hg) * hu).astype(MXU_DTYPE)
        out_ref[...] = jnp.dot(h, wdb[...], preferred_element_type=F32) * rw_ref[...]

    @pl.when(i >= nu_ref[0])
    def _():
        out_ref[...] = jnp.zeros_like(out_ref)


def _moe_ffn(block_e, n_used, xs, row_w, e_gate, e_up, e_down):
    n_rows, d = xs.shape
    bm = MOE_BLOCK
    ff = e_gate.shape[2]
    grid_spec = pltpu.PrefetchScalarGridSpec(
        num_scalar_prefetch=2,
        grid=(n_rows // bm,),
        in_specs=[pl.BlockSpec((bm, d), lambda i, be, nu: (i, 0)),
                  pl.BlockSpec((bm, 1), lambda i, be, nu: (i, 0)),
                  pl.BlockSpec((1, d, ff), lambda i, be, nu: (be[i], 0, 0)),
                  pl.BlockSpec((1, d, ff), lambda i, be, nu: (be[i], 0, 0)),
                  pl.BlockSpec((1, ff, d), lambda i, be, nu: (be[i], 0, 0))],
        out_specs=pl.BlockSpec((bm, d), lambda i, be, nu: (i, 0)),
        scratch_shapes=[pltpu.VMEM((d, ff), MXU_DTYPE), pltpu.VMEM((d, ff), MXU_DTYPE),
                        pltpu.VMEM((ff, d), MXU_DTYPE)],
    )
    return pl.pallas_call(
        _moe_ffn_kernel,
        grid_spec=grid_spec,
        out_shape=jax.ShapeDtypeStruct((n_rows, d), F32),
        compiler_params=_cparams("arbitrary"),
        name="moe_ffn",
    )(block_e, n_used, xs, row_w, e_gate, e_up, e_down)


def _ffn_out_kernel(x1_ref, xb_ref, yr_ref, mod_ref, sg_ref, su_ref, sd_ref, fg_ref, out_ref, *, final):
    xb = xb_ref[0]
    hg = jnp.dot(xb, sg_ref[...], preferred_element_type=F32)
    hu = jnp.dot(xb, su_ref[...], preferred_element_type=F32)
    y_sh = _dot(hg * _sigmoid(hg) * hu, sd_ref[...])
    x2 = x1_ref[0] + mod_ref[0, 0, 5:6, :] * (yr_ref[0] + y_sh)
    if final:
        x2 = _rms(x2, fg_ref[...])
    out_ref[0] = x2


def _ffn_out(x1, xb, y_routed, mod, sg, su, sd, final_g, ctx_tiles, final):
    b, s, d = x1.shape
    tm = TOKEN_TILE
    skip = ctx_tiles if final else 0
    n_tiles = s // tm - skip
    tok = lambda bi, i: (bi, i + skip, 0)
    full2 = lambda bi, i: (0, 0)
    return pl.pallas_call(
        functools.partial(_ffn_out_kernel, final=final),
        grid=(b, n_tiles),
        in_specs=[pl.BlockSpec((1, tm, d), tok),
                  pl.BlockSpec((1, tm, d), tok),
                  pl.BlockSpec((1, tm, d), tok),
                  pl.BlockSpec((1, 1, 8, d),
                               lambda bi, i: (bi, jnp.minimum((i + skip) // ctx_tiles, 1), 0, 0)),
                  pl.BlockSpec(sg.shape, full2),
                  pl.BlockSpec(su.shape, full2),
                  pl.BlockSpec(sd.shape, full2),
                  pl.BlockSpec((1, d), full2)],
        out_specs=pl.BlockSpec((1, tm, d), lambda bi, i: (bi, i, 0)),
        out_shape=jax.ShapeDtypeStruct((b, n_tiles * tm, d), F32),
        compiler_params=_cparams("parallel", "arbitrary"),
        name="ffn_out",
    )(x1, xb, y_routed, mod, sg, su, sd, final_g)


def _pos_embed_2d(rows, dim):
    quarter = dim // 4
    omega = 1.0 / (10000.0 ** (jnp.arange(quarter, dtype=F32) / quarter))
    row = jnp.repeat(jnp.arange(rows, dtype=F32), GRID_W)
    col = jnp.tile(jnp.arange(GRID_W, dtype=F32), rows)

    def axis_embed(p):
        ang = p[:, None] * omega[None, :]
        return jnp.concatenate([jnp.sin(ang), jnp.cos(ang)], axis=-1)

    return jnp.concatenate([axis_embed(row), axis_embed(col)], axis=-1)


def _pad_heads(w, n_heads, axis):
    shape = w.shape
    hd = shape[axis] // n_heads
    w = w.reshape(shape[:axis] + (n_heads, hd) + shape[axis + 1:])
    pad = [(0, 0)] * w.ndim
    pad[axis + 1] = (0, LANE - hd)
    w = jnp.pad(w, pad)
    return w.reshape(shape[:axis] + (n_heads * LANE,) + shape[axis + 1:])


def _dispatch(eidx, wts, n_exp, bm):
    t, k = eidx.shape
    a = t * k
    flat_e = eidx.reshape(a)
    order = jnp.argsort(flat_e)
    e_sorted = flat_e[order]
    counts = jnp.zeros((n_exp,), jnp.int32).at[flat_e].add(1)
    padded = (counts + bm - 1) // bm * bm
    pad_end = jnp.cumsum(padded)
    pad_start = pad_end - padded
    start = jnp.cumsum(counts) - counts
    dest_sorted = pad_start[e_sorted] + jnp.arange(a, dtype=jnp.int32) - start[e_sorted]
    n_blocks = -(-(a + n_exp * (bm - 1)) // bm)
    n_rows = n_blocks * bm
    row_tok = jnp.zeros((n_rows,), jnp.int32).at[dest_sorted].set((order // k).astype(jnp.int32))
    row_w = jnp.zeros((n_rows,), F32).at[dest_sorted].set(wts.reshape(a)[order])
    dest = jnp.zeros((a,), jnp.int32).at[order].set(dest_sorted).reshape(t, k)
    block_e = jnp.minimum(jnp.searchsorted(pad_end, jnp.arange(n_blocks, dtype=jnp.int32) * bm, side='right'),
                          n_exp - 1).astype(jnp.int32)
    n_used = (pad_end[-1] // bm).astype(jnp.int32).reshape(1)
    return row_tok, row_w, dest, block_e, n_used


def kernel(x, c, ctx, c_ctx, norm1_g, norm2_g, w_mod, b_mod, w_in, rg_conv_w, rg_conv_b, rg_gate_w, rg_gate_b, rg_lambda, mlstm_gate_b, mlstm_norm_g, cm_norm_g, cm_w, cm_b, w_out, router_w, router_b, exp_gate, exp_up, exp_down, sh_gate, sh_up, sh_down, final_norm_g):
    batch, seq, d = x.shape
    ctx_len = ctx.shape[1]
    depth = w_in.shape[0]
    rg_w = rg_conv_w.shape[2]
    m_w = mlstm_norm_g.shape[1]
    cm_width = cm_norm_g.shape[1]
    head_dim = m_w // M_HEADS
    rg_hd = rg_w // RG_HEADS
    n_exp = router_w.shape[2]
    n_gates = 4 * M_HEADS
    assert ctx_len % TOKEN_TILE == 0 and seq % TOKEN_TILE == 0
    assert rg_w % LANE == 0 and LANE % rg_hd == 0 and head_dim < LANE and cm_width % LANE == 0
    ctx_tiles = ctx_len // TOKEN_TILE
    s_all = ctx_len + seq
    n_tok = batch * s_all

    xall = jnp.concatenate([ctx, x + _pos_embed_2d(seq // GRID_W, d).astype(x.dtype)[None]], axis=1)
    cond = jnp.concatenate([c, c_ctx[None], jnp.zeros((SUBLANE - (batch + 1) % SUBLANE, d), F32)], axis=0)

    out = None
    for l in range(depth):
        last = l == depth - 1
        sizes = (rg_w, rg_w, m_w, m_w, m_w, m_w, n_gates, 2 * cm_width)
        offs = np.cumsum((0,) + sizes)
        w_cols = [w_in[l][:, offs[i]:offs[i + 1]] for i in range(len(sizes))]
        w_p = jnp.concatenate(
            [w_cols[0], w_cols[1]] + [_pad_heads(w_cols[i], M_HEADS, 1) for i in (2, 3, 4, 5)]
            + [jnp.pad(w_cols[6], ((0, 0), (0, LANE - n_gates))), w_cols[7]], axis=1).astype(MXU_DTYPE)
        splits = (2 * rg_w, 4 * M_HEADS * LANE, LANE, 2 * cm_width)
        hp = LANE // rg_hd
        gw = rg_gate_w[l].reshape(4, RG_HEADS // hp, hp, rg_hd, rg_hd)
        eye = jnp.eye(hp, dtype=F32)
        wg = jnp.einsum('gthij,hk->tghikj', gw, eye).reshape(RG_HEADS // hp, 4, LANE, LANE)
        wg = wg.transpose(0, 2, 1, 3).reshape(RG_HEADS // hp, LANE, 4 * LANE).astype(MXU_DTYPE)
        bg = rg_gate_b[l].reshape(4, RG_HEADS // hp, LANE).transpose(1, 0, 2).reshape(RG_HEADS // hp, 1, 4 * LANE)
        gate_b = jnp.pad(mlstm_gate_b[l].reshape(1, n_gates), ((0, 0), (0, LANE - n_gates)))
        m_norm_g = _pad_heads(mlstm_norm_g[l], M_HEADS, 0).reshape(M_HEADS, 1, LANE)
        cm_ws = cm_w[l].astype(MXU_DTYPE)
        cm_bmap = jnp.repeat(cm_b[l].T, cm_width // CM_GROUPS, axis=1)
        w_o = w_out[l]
        w_r = w_o[:rg_w].astype(MXU_DTYPE)
        w_m = _pad_heads(w_o[rg_w:rg_w + m_w], M_HEADS, 0).astype(MXU_DTYPE)
        w_c = w_o[rg_w + m_w:].astype(MXU_DTYPE)

        mod = _adaln(cond, w_mod[l], b_mod[l]).reshape(cond.shape[0], 6, d)
        mod = jnp.pad(mod, ((0, 0), (0, 2), (0, 0)))
        mod = jnp.stack([jnp.broadcast_to(mod[batch], (batch, 8, d)), mod[:batch]], axis=1)

        rg, qkvo, gates, cm = _inproj(xall, mod, norm1_g[l].reshape(1, d), w_p, splits, ctx_tiles)
        y_rg = _rglru(rg, rg_conv_w[l], rg_conv_b[l].reshape(1, rg_w), wg, bg, rg_lambda[l], ctx_len)
        y_m = _mlstm(qkvo, gates, gate_b, m_norm_g, ctx_len, head_dim)
        y_cm = _gmlp(cm, cm_norm_g[l].reshape(1, cm_width), cm_ws, cm_bmap)
        x1, xb, eidx, wts = _outproj_router(y_rg, y_m, y_cm, xall, mod, norm2_g[l].reshape(1, d),
                                            w_r, w_m, w_c, router_w[l], router_b[l].reshape(1, n_exp),
                                            ctx_tiles)

        eidx = eidx.reshape(n_tok, LANE)[:, :TOP_K]
        wts = wts.reshape(n_tok, LANE)[:, :TOP_K]
        row_tok, row_w, dest, block_e, n_used = _dispatch(eidx, wts, n_exp, MOE_BLOCK)
        xs = xb.reshape(n_tok, d)[row_tok]
        ys = _moe_ffn(block_e, n_used, xs, row_w.reshape(-1, 1), exp_gate[l], exp_up[l], exp_down[l])
        y_routed = ys[dest].sum(axis=1).reshape(batch, s_all, d)
        out = _ffn_out(x1, xb, y_routed, mod, sh_gate[l].astype(MXU_DTYPE), sh_up[l].astype(MXU_DTYPE),
                       sh_down[l].astype(MXU_DTYPE), final_norm_g.reshape(1, d), ctx_tiles, last)
        xall = out
    return out
```

```python
import functools

import numpy as np
import jax
import jax.numpy as jnp
from jax import lax
from jax.experimental import pallas as pl
from jax.experimental.pallas import tpu as pltpu

F32 = jnp.float32
MXU_DTYPE = jnp.bfloat16
LANE = 128
SUBLANE = 8
VMEM_LIMIT = 56 * 1024 * 1024

GRID_W = 64
RG_HEADS = 6
RG_C = 8.0
M_HEADS = 4
M_CHUNK = 128
CM_GROUPS = 4
CM_CHUNK = 128
N_GROUPS = 8
TOPK_GROUPS = 4
TOP_K = 8
ROUTED_SCALE = 2.5
EPS = 1e-6

TOKEN_TILE = 256
MOE_BLOCK = 256


def _cparams(*sem):
    return pltpu.CompilerParams(dimension_semantics=sem, vmem_limit_bytes=VMEM_LIMIT)


def _dot(a, b):
    return jnp.dot(a.astype(MXU_DTYPE), b.astype(MXU_DTYPE), preferred_element_type=F32)


def _sigmoid(x):
    return 1.0 / (1.0 + jnp.exp(-x))


def _gelu_tanh(x):
    return 0.5 * x * (1.0 + jnp.tanh(0.7978845608028654 * (x + 0.044715 * (x * x * x))))


def _softplus(x):
    return jnp.maximum(x, 0.0) + jnp.log(1.0 + jnp.exp(-jnp.abs(x)))


def _rms(x, g):
    return x * lax.rsqrt(jnp.mean(x * x, axis=-1, keepdims=True) + EPS) * g


def _adaln_kernel(c_ref, w_ref, b_ref, o_ref):
    c = c_ref[...]
    o_ref[...] = _dot(c * _sigmoid(c), w_ref[...]) + b_ref[...]


def _adaln(cond, w_mod, b_mod):
    n, d = cond.shape
    n_out = w_mod.shape[1]
    tn = 1536
    return pl.pallas_call(
        _adaln_kernel,
        grid=(n_out // tn,),
        in_specs=[pl.BlockSpec((n, d), lambda j: (0, 0)),
                  pl.BlockSpec((d, tn), lambda j: (0, j)),
                  pl.BlockSpec((1, tn), lambda j: (0, j))],
        out_specs=pl.BlockSpec((n, tn), lambda j: (0, j)),
        out_shape=jax.ShapeDtypeStruct((n, n_out), F32),
        compiler_params=_cparams("arbitrary"),
        name="adaln",
    )(cond, w_mod, b_mod.reshape(1, n_out))


def _inproj_kernel(x_ref, mod_ref, g_ref, w_ref, rg_ref, qkvo_ref, gt_ref, cm_ref, *, splits):
    x = x_ref[0]
    shift = mod_ref[0, 0, 0:1, :]
    scale = mod_ref[0, 0, 1:2, :]
    xa = (_rms(x, g_ref[...]) * (1.0 + scale) + shift).astype(MXU_DTYPE)
    a = 0
    for ref, width in zip((rg_ref, qkvo_ref, gt_ref, cm_ref), splits):
        ref[0] = jnp.dot(xa, w_ref[:, a:a + width], preferred_element_type=F32)
        a += width


def _inproj(xall, mod, g, w_p, splits, ctx_tiles):
    b, s, d = xall.shape
    tm = TOKEN_TILE
    n_p = w_p.shape[1]
    tok = lambda bi, i: (bi, i, 0)
    return pl.pallas_call(
        functools.partial(_inproj_kernel, splits=splits),
        grid=(b, s // tm),
        in_specs=[pl.BlockSpec((1, tm, d), tok),
                  pl.BlockSpec((1, 1, 8, d), lambda bi, i: (bi, jnp.minimum(i // ctx_tiles, 1), 0, 0)),
                  pl.BlockSpec((1, d), lambda bi, i: (0, 0)),
                  pl.BlockSpec((d, n_p), lambda bi, i: (0, 0))],
        out_specs=[pl.BlockSpec((1, tm, w), tok) for w in splits],
        out_shape=[jax.ShapeDtypeStruct((b, s, w), F32) for w in splits],
        compiler_params=_cparams("parallel", "arbitrary"),
        name="inproj",
    )(xall, mod, g, w_p)


def _rglru_kernel(rx_ref, rgx_ref, cw_ref, cb_ref, wg_ref, bg_ref, lam_ref, y_ref,
                  af, uf, ab, ub, *, ctx_len):
    s = rx_ref.shape[1]
    x = rx_ref[0]
    t = lax.broadcasted_iota(jnp.int32, (s, LANE), 0)
    is_lat = t >= ctx_len
    pos = jnp.where(is_lat, t - ctx_len, t)
    seg_len = jnp.where(is_lat, s - ctx_len, ctx_len)
    xm2 = jnp.where(pos >= 2, pltpu.roll(x, 2, 0), 0.0)
    xm1 = jnp.where(pos >= 1, pltpu.roll(x, 1, 0), 0.0)
    xp1 = jnp.where(pos < seg_len - 1, pltpu.roll(x, s - 1, 0), 0.0)
    cw = cw_ref[...]
    z = cw[0:1] * xm2 + cw[1:2] * xm1 + cw[2:3] * x + cw[3:4] * xp1 + cb_ref[...]
    pre = _dot(z, wg_ref[0]) + bg_ref[0]
    sp = _softplus(-lam_ref[...])
    for d, (a_ref, u_ref) in enumerate(((af, uf), (ab, ub))):
        r = _sigmoid(pre[:, (2 * d) * LANE:(2 * d + 1) * LANE])
        i = _sigmoid(pre[:, (2 * d + 1) * LANE:(2 * d + 2) * LANE])
        log_a = -RG_C * r * sp[d:d + 1]
        a_ref[...] = jnp.exp(log_a)
        u_ref[...] = jnp.sqrt(1.0 - jnp.exp(2.0 * log_a)) * (i * z)

    row = lax.broadcasted_iota(jnp.int32, (SUBLANE, LANE), 0)
    n_tiles = s // SUBLANE
    n_ctx = ctx_len // SUBLANE

    def body(n, carry):
        hf_prev, hb_prev = carry
        of = pl.multiple_of(n * SUBLANE, SUBLANE)
        a = af[pl.ds(of, SUBLANE), :]
        u = uf[pl.ds(of, SUBLANE), :]
        for sh in (1, 2, 4):
            a_sh = jnp.where(row >= sh, pltpu.roll(a, sh, 0), 1.0)
            u_sh = jnp.where(row >= sh, pltpu.roll(u, sh, 0), 0.0)
            u = a * u_sh + u
            a = a * a_sh
        h = a * hf_prev + u
        uf[pl.ds(of, SUBLANE), :] = h
        hf_new = h[SUBLANE - 1:SUBLANE, :]
        tb = jnp.where(n < n_ctx, n_ctx - 1 - n, n_tiles - 1 + n_ctx - n)
        ob = pl.multiple_of(tb * SUBLANE, SUBLANE)
        a = ab[pl.ds(ob, SUBLANE), :]
        u = ub[pl.ds(ob, SUBLANE), :]
        for sh in (1, 2, 4):
            a_sh = jnp.where(row < SUBLANE - sh, pltpu.roll(a, SUBLANE - sh, 0), 1.0)
            u_sh = jnp.where(row < SUBLANE - sh, pltpu.roll(u, SUBLANE - sh, 0), 0.0)
            u = a * u_sh + u
            a = a * a_sh
        h = a * hb_prev + u
        ub[pl.ds(ob, SUBLANE), :] = h
        return hf_new, h[0:1, :]

    zero = jnp.zeros((1, LANE), F32)
    lax.fori_loop(0, n_tiles, body, (zero, zero))
    y_ref[0] = (uf[...] + ub[...]) * _gelu_tanh(rgx_ref[0])


def _rglru(rg, conv_w, conv_b, wg, bg, lam, ctx_len):
    b, s, w2 = rg.shape
    nt = w2 // 2 // LANE
    return pl.pallas_call(
        functools.partial(_rglru_kernel, ctx_len=ctx_len),
        grid=(b, nt),
        in_specs=[pl.BlockSpec((1, s, LANE), lambda bi, j: (bi, 0, j)),
                  pl.BlockSpec((1, s, LANE), lambda bi, j: (bi, 0, nt + j)),
                  pl.BlockSpec((conv_w.shape[0], LANE), lambda bi, j: (0, j)),
                  pl.BlockSpec((1, LANE), lambda bi, j: (0, j)),
                  pl.BlockSpec((1, LANE, 4 * LANE), lambda bi, j: (j, 0, 0)),
                  pl.BlockSpec((1, 1, 4 * LANE), lambda bi, j: (j, 0, 0)),
                  pl.BlockSpec((2, LANE), lambda bi, j: (0, j))],
        out_specs=pl.BlockSpec((1, s, LANE), lambda bi, j: (bi, 0, j)),
        out_shape=jax.ShapeDtypeStruct((b, s, nt * LANE), F32),
        scratch_shapes=[pltpu.VMEM((s, LANE), F32)] * 4,
        compiler_params=_cparams("parallel", "arbitrary"),
        name="rglru",
    )(rg, rg, conv_w, conv_b, wg, bg, lam)


def _mlstm_kernel(q_ref, k_ref, v_ref, o_ref, g_ref, gb_ref, ng_ref, out_ref,
                  gs, hf, hb, *, ctx_len, head_dim):
    s = q_ref.shape[1]
    ln = M_CHUNK
    n_chunks = s // ln
    n_ctx = ctx_len // ln
    head = pl.program_id(1)
    lane = lax.broadcasted_iota(jnp.int32, (s, LANE), 1)
    tmod = lax.broadcasted_iota(jnp.int32, (s, LANE), 0) % ln

    gates = g_ref[0] + gb_ref[...]

    def col(c):
        return jnp.sum(jnp.where(lane == c, gates, 0.0), axis=1, keepdims=True)

    li_f = col(head)
    lf_f = -_softplus(-col(M_HEADS + head))
    li_b = col(2 * M_HEADS + head)
    lf_b = -_softplus(-col(3 * M_HEADS + head))
    acc = jnp.where(lane == 0, lf_f, 0.0) + jnp.where(lane == 2, lf_b, 0.0)
    sh = 1
    while sh < ln:
        down = jnp.where(tmod >= sh, pltpu.roll(acc, sh, 0), 0.0)
        up = jnp.where(tmod < ln - sh, pltpu.roll(acc, s - sh, 0), 0.0)
        acc = acc + jnp.where(lane == 0, down, up)
        sh *= 2
    gs[...] = acc + jnp.where(lane == 1, li_f, 0.0) + jnp.where(lane == 3, li_b, 0.0)

    tt = lax.broadcasted_iota(jnp.int32, (ln, ln), 0)
    ss = lax.broadcasted_iota(jnp.int32, (ln, ln), 1)
    lane_c = lax.broadcasted_iota(jnp.int32, (ln, LANE), 1)
    k_scale = head_dim ** -0.5

    def chunk_step(c, c_state, m_st, rev):
        o = pl.multiple_of(c * ln, ln)
        blk = gs[pl.ds(o, ln), :]
        bc = blk[:, 2 * rev:2 * rev + 1]
        li = blk[:, 2 * rev + 1:2 * rev + 2]
        w = li - bc
        w_row = jnp.transpose(jnp.broadcast_to(w, (ln, ln)))
        mask = (ss >= tt) if rev else (ss <= tt)
        dmat = jnp.where(mask, bc + w_row, -jnp.inf)
        inter = bc + m_st
        m = jnp.maximum(jnp.max(dmat, axis=1, keepdims=True), inter)
        p = jnp.exp(dmat - m)
        q = q_ref[0, pl.ds(o, ln), :].astype(MXU_DTYPE)
        kf = k_ref[0, pl.ds(o, ln), :] * k_scale
        v = jnp.where(lane_c == head_dim, 1.0, v_ref[0, pl.ds(o, ln), :]).astype(MXU_DTYPE)
        sc = lax.dot_general(q, kf.astype(MXU_DTYPE), (((1,), (1,)), ((), ())),
                             preferred_element_type=F32) * p
        e_inter = jnp.exp(inter - m)
        num = (jnp.dot(sc.astype(MXU_DTYPE), v, preferred_element_type=F32)
               + e_inter * jnp.dot(q, c_state.astype(MXU_DTYPE), preferred_element_type=F32))
        den = jnp.sum(jnp.where(lane_c == head_dim, num, 0.0), axis=1, keepdims=True)
        h = num / jnp.maximum(jnp.abs(den), jnp.exp(-m))
        g = bc[0:1, :] if rev else bc[ln - 1:ln, :]
        w_end = g + w
        m_new = jnp.maximum(g + m_st, jnp.max(w_end, axis=0, keepdims=True))
        decay = jnp.exp(g + m_st - m_new)
        ks = (kf * jnp.exp(w_end - m_new)).astype(MXU_DTYPE)
        c_new = decay * c_state + lax.dot_general(ks, v, (((0,), (0,)), ((), ())),
                                                  preferred_element_type=F32)
        return h, c_new, m_new

    def body(n, carry):
        cf, mf, cb, mb = carry
        h, cf, mf = chunk_step(n, cf, mf, 0)
        hf[pl.ds(pl.multiple_of(n * ln, ln), ln), :] = h
        nb = jnp.where(n < n_ctx, n_ctx - 1 - n, n_chunks - 1 + n_ctx - n)
        h, cb, mb = chunk_step(nb, cb, mb, 1)
        hb[pl.ds(pl.multiple_of(nb * ln, ln), ln), :] = h
        return cf, mf, cb, mb

    c0 = jnp.zeros((LANE, LANE), F32)
    m0 = jnp.zeros((1, 1), F32)
    lax.fori_loop(0, n_chunks, body, (c0, m0, c0, m0))

    hsum = jnp.where(lane < head_dim, hf[...] + hb[...], 0.0)
    ms = jnp.sum(hsum * hsum, axis=1, keepdims=True) * (1.0 / head_dim)
    out_ref[0] = _sigmoid(o_ref[0]) * (hsum * lax.rsqrt(ms + EPS) * ng_ref[0])


def _mlstm(qkvo, gates, gate_b, norm_g, ctx_len, head_dim):
    b, s, _ = qkvo.shape
    h = M_HEADS
    blk = lambda off: pl.BlockSpec((1, s, LANE), lambda bi, hi: (bi, 0, off * h + hi))
    return pl.pallas_call(
        functools.partial(_mlstm_kernel, ctx_len=ctx_len, head_dim=head_dim),
        grid=(b, h),
        in_specs=[blk(0), blk(1), blk(2), blk(3),
                  pl.BlockSpec((1, s, LANE), lambda bi, hi: (bi, 0, 0)),
                  pl.BlockSpec((1, LANE), lambda bi, hi: (0, 0)),
                  pl.BlockSpec((1, 1, LANE), lambda bi, hi: (hi, 0, 0))],
        out_specs=pl.BlockSpec((1, s, LANE), lambda bi, hi: (bi, 0, hi)),
        out_shape=jax.ShapeDtypeStruct((b, s, h * LANE), F32),
        scratch_shapes=[pltpu.VMEM((s, LANE), F32)] * 3,
        compiler_params=_cparams("parallel", "arbitrary"),
        name="mlstm",
    )(qkvo, qkvo, qkvo, qkvo, gates, gate_b, norm_g)


def _gmlp_kernel(cm_ref, ng_ref, ws_ref, bs_ref, y_ref, *, width):
    tm = cm_ref.shape[1]
    gd = width // CM_GROUPS
    lane = lax.broadcasted_iota(jnp.int32, (CM_CHUNK, width), 1)
    for c in range(tm // CM_CHUNK):
        rows = slice(c * CM_CHUNK, (c + 1) * CM_CHUNK)
        u = _gelu_tanh(cm_ref[0, rows, 0:width])
        v = _rms(_gelu_tanh(cm_ref[0, rows, width:2 * width]), ng_ref[...]).astype(MXU_DTYPE)
        mixed = bs_ref[...]
        for g in range(CM_GROUPS):
            mg = jnp.dot(ws_ref[g], v, preferred_element_type=F32)
            mixed = mixed + jnp.where((lane >= g * gd) & (lane < (g + 1) * gd), mg, 0.0)
        y_ref[0, rows, :] = u * mixed


def _gmlp(cm, norm_g, w_s, b_map):
    b, s, w2 = cm.shape
    width = w2 // 2
    tm = TOKEN_TILE
    return pl.pallas_call(
        functools.partial(_gmlp_kernel, width=width),
        grid=(b, s // tm),
        in_specs=[pl.BlockSpec((1, tm, w2), lambda bi, i: (bi, i, 0)),
                  pl.BlockSpec((1, width), lambda bi, i: (0, 0)),
                  pl.BlockSpec(w_s.shape, lambda bi, i: (0, 0, 0)),
                  pl.BlockSpec(b_map.shape, lambda bi, i: (0, 0))],
        out_specs=pl.BlockSpec((1, tm, width), lambda bi, i: (bi, i, 0)),
        out_shape=jax.ShapeDtypeStruct((b, s, width), F32),
        compiler_params=_cparams("parallel", "arbitrary"),
        name="gmlp",
    )(cm, norm_g, w_s, b_map)


def _outproj_router_kernel(yr_ref, ym_ref, yc_ref, x_ref, mod_ref, g_ref, wr_ref, wm_ref, wc_ref,
                           rw_ref, rb_ref, x1_ref, xb_ref, xrow_ref, idx_ref, wt_ref, rank_ref, cnt_ref,
                           counts):
    @pl.when((pl.program_id(0) == 0) & (pl.program_id(1) == 0))
    def _():
        counts[...] = jnp.zeros_like(counts)

    mix = (_dot(yr_ref[0], wr_ref[...]) + _dot(ym_ref[0], wm_ref[...]) + _dot(yc_ref[0], wc_ref[...]))
    x1 = x_ref[0] + mod_ref[0, 0, 2:3, :] * mix
    x1_ref[0] = x1
    xb = _rms(x1, g_ref[...]) * (1.0 + mod_ref[0, 0, 4:5, :]) + mod_ref[0, 0, 3:4, :]
    xb_ref[0] = xb.astype(xb_ref.dtype)
    for j in range(xb.shape[1] // LANE):
        xrow_ref[0, pl.ds(j, xb.shape[0], stride=SUBLANE), :] = xb[:, j * LANE:(j + 1) * LANE]

    logits = jnp.dot(xb, rw_ref[...], preferred_element_type=F32, precision=lax.Precision.HIGHEST)
    scores = _sigmoid(logits)
    sel = scores + rb_ref[...]
    tm, n_exp = sel.shape
    per_group = n_exp // N_GROUPS
    neg = -jnp.inf
    lane_h = lax.broadcasted_iota(jnp.int32, (tm, LANE), 1)
    lane_hf = lane_h.astype(F32)
    lane_e = lax.broadcasted_iota(jnp.int32, (tm, n_exp), 1)
    lane_ef = lane_e.astype(F32)
    groups_per_tile = LANE // per_group

    gscore = jnp.full((tm, LANE), neg, F32)
    for g in range(N_GROUPS):
        tile = sel[:, (g // groups_per_tile) * LANE:(g // groups_per_tile + 1) * LANE]
        mg = jnp.where(lane_h // per_group == g % groups_per_tile, tile, neg)
        m1 = jnp.max(mg, axis=1, keepdims=True)
        i1 = jnp.min(jnp.where(mg == m1, lane_hf, float(LANE)), axis=1, keepdims=True)
        m2 = jnp.max(jnp.where(lane_hf == i1, neg, mg), axis=1, keepdims=True)
        gscore = jnp.where(lane_h == g, m1 + m2, gscore)
    group_of = (lane_e // per_group).astype(F32)
    allowed = jnp.zeros((tm, n_exp), jnp.bool_)
    for _ in range(TOPK_GROUPS):
        m = jnp.max(gscore, axis=1, keepdims=True)
        gi = jnp.min(jnp.where(gscore == m, lane_hf, float(LANE)), axis=1, keepdims=True)
        gscore = jnp.where(lane_hf == gi, neg, gscore)
        allowed = allowed | (group_of == gi)
    cand = jnp.where(allowed, sel, neg)
    idx_out = jnp.zeros((tm, LANE), F32)
    w_out = jnp.zeros((tm, LANE), F32)
    chosen = jnp.zeros((tm, n_exp), F32)
    picks = []
    for k in range(TOP_K):
        m = jnp.max(cand, axis=1, keepdims=True)
        ik = jnp.min(jnp.where(cand == m, lane_ef, float(n_exp)), axis=1, keepdims=True)
        hit = lane_ef == ik
        wk = jnp.sum(jnp.where(hit, scores, 0.0), axis=1, keepdims=True)
        cand = jnp.where(hit, neg, cand)
        chosen = jnp.where(hit, 1.0, chosen)
        picks.append(ik)
        idx_out = jnp.where(lane_h == k, ik, idx_out)
        w_out = jnp.where(lane_h == k, wk, w_out)
    w_sum = jnp.sum(w_out, axis=1, keepdims=True)
    idx_ref[0] = idx_out.astype(jnp.int32)
    wt_ref[0] = w_out / w_sum * ROUTED_SCALE

    tt = lax.broadcasted_iota(jnp.int32, (tm, tm), 0)
    ss = lax.broadcasted_iota(jnp.int32, (tm, tm), 1)
    before = jnp.dot((ss < tt).astype(MXU_DTYPE), chosen.astype(MXU_DTYPE), preferred_element_type=F32)
    before = before + counts[...]
    rank_out = jnp.zeros((tm, LANE), F32)
    for k in range(TOP_K):
        rk = jnp.sum(jnp.where(lane_ef == picks[k], before, 0.0), axis=1, keepdims=True)
        rank_out = jnp.where(lane_h == k, rk, rank_out)
    rank_ref[0] = rank_out.astype(jnp.int32)
    counts[...] = counts[...] + jnp.sum(chosen, axis=0, keepdims=True)
    cnt_ref[...] = counts[...].astype(jnp.int32)


def _outproj_router(y_rg, y_m, y_cm, xall, mod, g2, w_r, w_m, w_c, router_w, router_b, ctx_tiles):
    b, s, d = xall.shape
    tm = TOKEN_TILE
    n_exp = router_w.shape[1]
    tok = lambda bi, i: (bi, i, 0)
    full2 = lambda bi, i: (0, 0)
    return pl.pallas_call(
        _outproj_router_kernel,
        grid=(b, s // tm),
        in_specs=[pl.BlockSpec((1, tm, y_rg.shape[2]), tok),
                  pl.BlockSpec((1, tm, y_m.shape[2]), tok),
                  pl.BlockSpec((1, tm, y_cm.shape[2]), tok),
                  pl.BlockSpec((1, tm, d), tok),
                  pl.BlockSpec((1, 1, 8, d), lambda bi, i: (bi, jnp.minimum(i // ctx_tiles, 1), 0, 0)),
                  pl.BlockSpec((1, d), full2),
                  pl.BlockSpec(w_r.shape, full2),
                  pl.BlockSpec(w_m.shape, full2),
                  pl.BlockSpec(w_c.shape, full2),
                  pl.BlockSpec(router_w.shape, full2),
                  pl.BlockSpec((1, router_w.shape[1]), full2)],
        out_specs=[pl.BlockSpec((1, tm, d), tok),
                   pl.BlockSpec((1, tm, d), tok),
                   pl.BlockSpec((1, tm * SUBLANE, LANE), tok),
                   pl.BlockSpec((1, tm, LANE), tok),
                   pl.BlockSpec((1, tm, LANE), tok),
                   pl.BlockSpec((1, tm, LANE), tok),
                   pl.BlockSpec((1, n_exp), full2)],
        out_shape=[jax.ShapeDtypeStruct((b, s, d), F32),
                   jax.ShapeDtypeStruct((b, s, d), MXU_DTYPE),
                   jax.ShapeDtypeStruct((b, s * SUBLANE, LANE), F32),
                   jax.ShapeDtypeStruct((b, s, LANE), jnp.int32),
                   jax.ShapeDtypeStruct((b, s, LANE), F32),
                   jax.ShapeDtypeStruct((b, s, LANE), jnp.int32),
                   jax.ShapeDtypeStruct((1, n_exp), jnp.int32)],
        scratch_shapes=[pltpu.VMEM((1, n_exp), F32)],
        compiler_params=_cparams("arbitrary", "arbitrary"),
        name="outproj_router",
    )(y_rg, y_m, y_cm, xall, mod, g2, w_r, w_m, w_c, router_w, router_b)


def _row_slab(ref, row):
    return ref.at[pl.ds(pl.multiple_of(row * SUBLANE, SUBLANE), SUBLANE), :]


def _dispatch_kernel(zb_ref, nz_ref, dest_ref, xrow_ref, xs_hbm, zbuf, sem, zsem):
    tm = xrow_ref.shape[1] // SUBLANE
    zrows = zbuf.shape[0]

    @pl.when((pl.program_id(0) == 0) & (pl.program_id(1) == 0))
    def _():
        zbuf[...] = jnp.zeros_like(zbuf)

        def z_start(n, carry):
            dst = xs_hbm.at[pl.ds(pl.multiple_of(zb_ref[n] * zrows, zrows), zrows), :]
            pltpu.make_async_copy(zbuf, dst, zsem).start()
            return carry

        def z_wait(n, carry):
            pltpu.make_async_copy(zbuf, xs_hbm.at[pl.ds(0, zrows), :], zsem).wait()
            return carry

        lax.fori_loop(0, nz_ref[0], z_start, 0)
        lax.fori_loop(0, nz_ref[0], z_wait, 0)

    def issue(r, carry):
        src = xrow_ref.at[0, pl.ds(pl.multiple_of(r * SUBLANE, SUBLANE), SUBLANE), :]
        for k in range(TOP_K):
            pltpu.make_async_copy(src, _row_slab(xs_hbm, dest_ref[0, 0, r * TOP_K + k]), sem).start()
        return carry

    lax.fori_loop(0, tm, issue, 0)
    n_all = tm * TOP_K * SUBLANE
    pltpu.make_async_copy(xs_hbm.at[pl.ds(0, n_all), :], xs_hbm.at[pl.ds(0, n_all), :], sem).wait()


def _dispatch(pad_blocks, n_pad_blocks, dest_tiles, xrow, n_rows):
    b, s8, _ = xrow.shape
    tm = TOKEN_TILE
    tiles = s8 // SUBLANE // tm
    grid_spec = pltpu.PrefetchScalarGridSpec(
        num_scalar_prefetch=2,
        grid=(b, tiles),
        in_specs=[pl.BlockSpec((1, 1, tm * TOP_K), lambda bi, i, zb, nz: (bi * tiles + i, 0, 0),
                               memory_space=pltpu.SMEM),
                  pl.BlockSpec((1, tm * SUBLANE, LANE), lambda bi, i, zb, nz: (bi, i, 0))],
        out_specs=pl.BlockSpec(memory_space=pl.ANY),
        scratch_shapes=[pltpu.VMEM((MOE_BLOCK * SUBLANE, LANE), F32),
                        pltpu.SemaphoreType.DMA(()), pltpu.SemaphoreType.DMA(())],
    )
    return pl.pallas_call(
        _dispatch_kernel,
        grid_spec=grid_spec,
        out_shape=jax.ShapeDtypeStruct((n_rows * SUBLANE, LANE), F32),
        compiler_params=_cparams("arbitrary", "arbitrary"),
        name="dispatch",
    )(pad_blocks, n_pad_blocks, dest_tiles, xrow)


def _moe_ffn_kernel(be_ref, nu_ref, xs_ref, wg_ref, wu_ref, wd_ref, out_ref, wgb, wub, wdb):
    i = pl.program_id(0)
    bm = xs_ref.shape[0] // SUBLANE
    first = (i == 0) | (be_ref[i] != be_ref[jnp.maximum(i - 1, 0)])

    @pl.when(first)
    def _():
        wgb[...] = wg_ref[0].astype(MXU_DTYPE)
        wub[...] = wu_ref[0].astype(MXU_DTYPE)
        wdb[...] = wd_ref[0].astype(MXU_DTYPE)

    @pl.when(i < nu_ref[0])
    def _():
        x = jnp.concatenate([xs_ref[pl.ds(j, bm, stride=SUBLANE), :] for j in range(SUBLANE)],
                            axis=1).astype(MXU_DTYPE)
        hg = jnp.dot(x, wgb[...], preferred_element_type=F32)
        hu = jnp.dot(x, wub[...], preferred_element_type=F32)
        h = (hg * _sigmoid(hg) * hu).astype(MXU_DTYPE)
        y = jnp.dot(h, wdb[...], preferred_element_type=F32)
        for j in range(SUBLANE):
            out_ref[pl.ds(j, bm, stride=SUBLANE), :] = y[:, j * LANE:(j + 1) * LANE]

    @pl.when(i >= nu_ref[0])
    def _():
        out_ref[...] = jnp.zeros_like(out_ref)


def _moe_ffn(block_e, n_used, xs, e_gate, e_up, e_down):
    bm = MOE_BLOCK
    n_blocks = xs.shape[0] // SUBLANE // bm
    _, d, ff = e_gate.shape
    used = lambda i, be, nu: (jnp.minimum(i, nu[0] - 1), 0)
    grid_spec = pltpu.PrefetchScalarGridSpec(
        num_scalar_prefetch=2,
        grid=(n_blocks,),
        in_specs=[pl.BlockSpec((bm * SUBLANE, LANE), used),
                  pl.BlockSpec((1, d, ff), lambda i, be, nu: (be[i], 0, 0)),
                  pl.BlockSpec((1, d, ff), lambda i, be, nu: (be[i], 0, 0)),
                  pl.BlockSpec((1, ff, d), lambda i, be, nu: (be[i], 0, 0))],
        out_specs=pl.BlockSpec((bm * SUBLANE, LANE), lambda i, be, nu: (i, 0)),
        scratch_shapes=[pltpu.VMEM((d, ff), MXU_DTYPE), pltpu.VMEM((d, ff), MXU_DTYPE),
                        pltpu.VMEM((ff, d), MXU_DTYPE)],
    )
    return pl.pallas_call(
        _moe_ffn_kernel,
        grid_spec=grid_spec,
        out_shape=jax.ShapeDtypeStruct(xs.shape, F32),
        compiler_params=_cparams("arbitrary"),
        name="moe_ffn",
    )(block_e, n_used, xs, e_gate, e_up, e_down)


def _ffn_out_kernel(dest_ref, x1_ref, xb_ref, wt_ref, mod_ref, sg_ref, su_ref, sd_ref, fg_ref, ys_hbm,
                    out_ref, buf, sem, *, final):
    tm = x1_ref.shape[1]

    def issue(r, carry):
        for k in range(TOP_K):
            pltpu.make_async_copy(_row_slab(ys_hbm, dest_ref[0, 0, r * TOP_K + k]),
                                  _row_slab(buf, k * tm + r), sem).start()
        return carry

    lax.fori_loop(0, tm, issue, 0)
    xb = xb_ref[0]
    hg = jnp.dot(xb, sg_ref[...], preferred_element_type=F32)
    hu = jnp.dot(xb, su_ref[...], preferred_element_type=F32)
    y_sh = _dot(hg * _sigmoid(hg) * hu, sd_ref[...])
    pltpu.make_async_copy(ys_hbm.at[pl.ds(0, buf.shape[0]), :], buf, sem).wait()
    w = wt_ref[0]
    cols = []
    for j in range(SUBLANE):
        acc = w[:, 0:1] * buf[pl.ds(j, tm, stride=SUBLANE), :]
        for k in range(1, TOP_K):
            acc = acc + w[:, k:k + 1] * buf[pl.ds(k * tm * SUBLANE + j, tm, stride=SUBLANE), :]
        cols.append(acc)
    x2 = x1_ref[0] + mod_ref[0, 0, 5:6, :] * (jnp.concatenate(cols, axis=1) + y_sh)
    if final:
        x2 = _rms(x2, fg_ref[...])
    out_ref[0] = x2


def _ffn_out(dest_tiles, x1, xb, wts, ys, mod, sg, su, sd, final_g, ctx_tiles, final):
    b, s, d = x1.shape
    tm = TOKEN_TILE
    tiles = s // tm
    skip = ctx_tiles if final else 0
    n_tiles = tiles - skip
    tok = lambda bi, i: (bi, i + skip, 0)
    full2 = lambda bi, i: (0, 0)
    return pl.pallas_call(
        functools.partial(_ffn_out_kernel, final=final),
        grid=(b, n_tiles),
        in_specs=[pl.BlockSpec((1, 1, tm * TOP_K), lambda bi, i: (bi * tiles + i + skip, 0, 0),
                               memory_space=pltpu.SMEM),
                  pl.BlockSpec((1, tm, d), tok),
                  pl.BlockSpec((1, tm, d), tok),
                  pl.BlockSpec((1, tm, LANE), tok),
                  pl.BlockSpec((1, 1, 8, d),
                               lambda bi, i: (bi, jnp.minimum((i + skip) // ctx_tiles, 1), 0, 0)),
                  pl.BlockSpec(sg.shape, full2),
                  pl.BlockSpec(su.shape, full2),
                  pl.BlockSpec(sd.shape, full2),
                  pl.BlockSpec((1, d), full2),
                  pl.BlockSpec(memory_space=pl.ANY)],
        out_specs=pl.BlockSpec((1, tm, d), lambda bi, i: (bi, i, 0)),
        out_shape=jax.ShapeDtypeStruct((b, n_tiles * tm, d), F32),
        scratch_shapes=[pltpu.VMEM((TOP_K * tm * SUBLANE, LANE), F32), pltpu.SemaphoreType.DMA(())],
        compiler_params=_cparams("parallel", "arbitrary"),
        name="ffn_out",
    )(dest_tiles, x1, xb, wts, mod, sg, su, sd, final_g, ys)


def _pos_embed_2d(rows, dim):
    quarter = dim // 4
    omega = 1.0 / (10000.0 ** (jnp.arange(quarter, dtype=F32) / quarter))
    row = jnp.repeat(jnp.arange(rows, dtype=F32), GRID_W)
    col = jnp.tile(jnp.arange(GRID_W, dtype=F32), rows)

    def axis_embed(p):
        ang = p[:, None] * omega[None, :]
        return jnp.concatenate([jnp.sin(ang), jnp.cos(ang)], axis=-1)

    return jnp.concatenate([axis_embed(row), axis_embed(col)], axis=-1)


def _pad_heads(w, n_heads, axis):
    shape = w.shape
    hd = shape[axis] // n_heads
    w = w.reshape(shape[:axis] + (n_heads, hd) + shape[axis + 1:])
    pad = [(0, 0)] * w.ndim
    pad[axis + 1] = (0, LANE - hd)
    w = jnp.pad(w, pad)
    return w.reshape(shape[:axis] + (n_heads * LANE,) + shape[axis + 1:])


def _routing_tables(eidx, rank, counts, bm):
    t, k = eidx.shape
    n_exp = counts.shape[0]
    padded = (counts + bm - 1) // bm * bm
    pad_end = jnp.cumsum(padded)
    pad_start = pad_end - padded
    n_blocks = -(-(t * k + n_exp * (bm - 1)) // bm)
    dest = pad_start[eidx] + rank
    block_start = jnp.arange(n_blocks, dtype=jnp.int32) * bm
    block_e = jnp.minimum(jnp.sum(pad_end[None, :] <= block_start[:, None], axis=1), n_exp - 1)
    n_used = (pad_end[-1] // bm).reshape(1)
    has_pad = ((pad_end[block_e] == block_start + bm) & (counts[block_e] % bm != 0)) | (block_start >= pad_end[-1])
    slot = jnp.cumsum(has_pad) - 1
    block_id = jnp.arange(n_blocks, dtype=jnp.int32)
    pad_blocks = jnp.sum(jnp.where(has_pad[None, :] & (slot[None, :] == block_id[:, None]), block_id[None, :], 0),
                         axis=1)
    n_pad_blocks = jnp.sum(has_pad).reshape(1)
    return (dest.astype(jnp.int32), block_e.astype(jnp.int32), n_used.astype(jnp.int32),
            pad_blocks.astype(jnp.int32), n_pad_blocks.astype(jnp.int32), n_blocks * bm)


def kernel(x, c, ctx, c_ctx, norm1_g, norm2_g, w_mod, b_mod, w_in, rg_conv_w, rg_conv_b, rg_gate_w, rg_gate_b, rg_lambda, mlstm_gate_b, mlstm_norm_g, cm_norm_g, cm_w, cm_b, w_out, router_w, router_b, exp_gate, exp_up, exp_down, sh_gate, sh_up, sh_down, final_norm_g):
    batch, seq, d = x.shape
    ctx_len = ctx.shape[1]
    depth = w_in.shape[0]
    rg_w = rg_conv_w.shape[2]
    m_w = mlstm_norm_g.shape[1]
    cm_width = cm_norm_g.shape[1]
    head_dim = m_w // M_HEADS
    rg_hd = rg_w // RG_HEADS
    n_exp = router_w.shape[2]
    n_gates = 4 * M_HEADS
    assert ctx_len % TOKEN_TILE == 0 and seq % TOKEN_TILE == 0
    assert rg_w % LANE == 0 and LANE % rg_hd == 0 and head_dim < LANE and cm_width % LANE == 0
    assert d == SUBLANE * LANE
    ctx_tiles = ctx_len // TOKEN_TILE
    s_all = ctx_len + seq
    n_tok = batch * s_all

    xall = jnp.concatenate([ctx, x + _pos_embed_2d(seq // GRID_W, d).astype(x.dtype)[None]], axis=1)
    cond = jnp.concatenate([c, c_ctx[None], jnp.zeros((SUBLANE - (batch + 1) % SUBLANE, d), F32)], axis=0)

    out = None
    for l in range(depth):
        last = l == depth - 1
        sizes = (rg_w, rg_w, m_w, m_w, m_w, m_w, n_gates, 2 * cm_width)
        offs = np.cumsum((0,) + sizes)
        w_cols = [w_in[l][:, offs[i]:offs[i + 1]] for i in range(len(sizes))]
        w_p = jnp.concatenate(
            [w_cols[0], w_cols[1]] + [_pad_heads(w_cols[i], M_HEADS, 1) for i in (2, 3, 4, 5)]
            + [jnp.pad(w_cols[6], ((0, 0), (0, LANE - n_gates))), w_cols[7]], axis=1).astype(MXU_DTYPE)
        splits = (2 * rg_w, 4 * M_HEADS * LANE, LANE, 2 * cm_width)
        hp = LANE // rg_hd
        gw = rg_gate_w[l].reshape(4, RG_HEADS // hp, hp, rg_hd, rg_hd)
        eye = jnp.eye(hp, dtype=F32)
        wg = jnp.einsum('gthij,hk->tghikj', gw, eye).reshape(RG_HEADS // hp, 4, LANE, LANE)
        wg = wg.transpose(0, 2, 1, 3).reshape(RG_HEADS // hp, LANE, 4 * LANE).astype(MXU_DTYPE)
        bg = rg_gate_b[l].reshape(4, RG_HEADS // hp, LANE).transpose(1, 0, 2).reshape(RG_HEADS // hp, 1, 4 * LANE)
        gate_b = jnp.pad(mlstm_gate_b[l].reshape(1, n_gates), ((0, 0), (0, LANE - n_gates)))
        m_norm_g = _pad_heads(mlstm_norm_g[l], M_HEADS, 0).reshape(M_HEADS, 1, LANE)
        cm_ws = cm_w[l].astype(MXU_DTYPE)
        cm_bmap = jnp.repeat(cm_b[l].T, cm_width // CM_GROUPS, axis=1)
        w_o = w_out[l]
        w_r = w_o[:rg_w].astype(MXU_DTYPE)
        w_m = _pad_heads(w_o[rg_w:rg_w + m_w], M_HEADS, 0).astype(MXU_DTYPE)
        w_c = w_o[rg_w + m_w:].astype(MXU_DTYPE)

        mod = _adaln(cond, w_mod[l], b_mod[l]).reshape(cond.shape[0], 6, d)
        mod = jnp.pad(mod, ((0, 0), (0, 2), (0, 0)))
        mod = jnp.stack([jnp.broadcast_to(mod[batch], (batch, 8, d)), mod[:batch]], axis=1)

        rg, qkvo, gates, cm = _inproj(xall, mod, norm1_g[l].reshape(1, d), w_p, splits, ctx_tiles)
        y_rg = _rglru(rg, rg_conv_w[l], rg_conv_b[l].reshape(1, rg_w), wg, bg, rg_lambda[l], ctx_len)
        y_m = _mlstm(qkvo, gates, gate_b, m_norm_g, ctx_len, head_dim)
        y_cm = _gmlp(cm, cm_norm_g[l].reshape(1, cm_width), cm_ws, cm_bmap)
        x1, xb, xrow, eidx, wts, rank, counts = _outproj_router(
            y_rg, y_m, y_cm, xall, mod, norm2_g[l].reshape(1, d), w_r, w_m, w_c, router_w[l],
            router_b[l].reshape(1, n_exp), ctx_tiles)

        dest, block_e, n_used, pad_blocks, n_pad_blocks, n_rows = _routing_tables(
            eidx.reshape(n_tok, LANE)[:, :TOP_K], rank.reshape(n_tok, LANE)[:, :TOP_K], counts[0], MOE_BLOCK)
        dest_tiles = dest.reshape(n_tok // TOKEN_TILE, 1, TOKEN_TILE * TOP_K)
        xs = _dispatch(pad_blocks, n_pad_blocks, dest_tiles, xrow, n_rows)
        ys = _moe_ffn(block_e, n_used, xs, exp_gate[l], exp_up[l], exp_down[l])
        out = _ffn_out(dest_tiles, x1, xb, wts, ys, mod, sh_gate[l].astype(MXU_DTYPE),
                       sh_up[l].astype(MXU_DTYPE), sh_down[l].astype(MXU_DTYPE),
                       final_norm_g.reshape(1, d), ctx_tiles, last)
        xall = out
    return out
```

```python
import functools

import numpy as np
import jax
import jax.numpy as jnp
from jax import lax
from jax.experimental import pallas as pl
from jax.experimental.pallas import tpu as pltpu

F32 = jnp.float32
MXU_DTYPE = jnp.bfloat16
LANE = 128
SUBLANE = 8
VMEM_LIMIT = 56 * 1024 * 1024

GRID_W = 64
RG_HEADS = 6
RG_C = 8.0
M_HEADS = 4
M_CHUNK = 128
CM_GROUPS = 4
CM_CHUNK = 128
N_GROUPS = 8
TOPK_GROUPS = 4
TOP_K = 8
ROUTED_SCALE = 2.5
EPS = 1e-6

TOKEN_TILE = 256
MOE_BLOCK = 256


def _cparams(*sem):
    return pltpu.CompilerParams(dimension_semantics=sem, vmem_limit_bytes=VMEM_LIMIT)


def _dot(a, b):
    return jnp.dot(a.astype(MXU_DTYPE), b.astype(MXU_DTYPE), preferred_element_type=F32)


def _sigmoid(x):
    return 1.0 / (1.0 + jnp.exp(-x))


def _gelu_tanh(x):
    return 0.5 * x * (1.0 + jnp.tanh(0.7978845608028654 * (x + 0.044715 * (x * x * x))))


def _softplus(x):
    return jnp.maximum(x, 0.0) + jnp.log(1.0 + jnp.exp(-jnp.abs(x)))


def _rms(x, g):
    return x * lax.rsqrt(jnp.mean(x * x, axis=-1, keepdims=True) + EPS) * g


def _adaln_kernel(c_ref, w_ref, b_ref, o_ref):
    c = c_ref[...]
    o_ref[...] = _dot(c * _sigmoid(c), w_ref[0]) + b_ref[0]


def _adaln(cond, w_mod, b_mod, layer):
    n, d = cond.shape
    depth, _, n_out = w_mod.shape
    tn = 1536
    return pl.pallas_call(
        _adaln_kernel,
        grid=(n_out // tn,),
        in_specs=[pl.BlockSpec((n, d), lambda j: (0, 0)),
                  pl.BlockSpec((1, d, tn), lambda j: (layer, 0, j)),
                  pl.BlockSpec((1, 1, tn), lambda j: (layer, 0, j))],
        out_specs=pl.BlockSpec((n, tn), lambda j: (0, j)),
        out_shape=jax.ShapeDtypeStruct((n, n_out), F32),
        compiler_params=_cparams("arbitrary"),
        name="adaln",
    )(cond, w_mod, b_mod.reshape(depth, 1, n_out))


def _inproj_kernel(x_ref, mod_ref, g_ref, w_ref, rg_ref, qkvo_ref, gt_ref, cm_ref, *, splits):
    x = x_ref[0]
    shift = mod_ref[0, 0, 0:1, :]
    scale = mod_ref[0, 0, 1:2, :]
    xa = (_rms(x, g_ref[...]) * (1.0 + scale) + shift).astype(MXU_DTYPE)
    a = 0
    for ref, width in zip((rg_ref, qkvo_ref, gt_ref, cm_ref), splits):
        ref[0] = jnp.dot(xa, w_ref[:, a:a + width], preferred_element_type=F32)
        a += width


def _inproj(xall, mod, g, w_p, splits, ctx_tiles):
    b, s, d = xall.shape
    tm = TOKEN_TILE
    n_p = w_p.shape[1]
    tok = lambda bi, i: (bi, i, 0)
    return pl.pallas_call(
        functools.partial(_inproj_kernel, splits=splits),
        grid=(b, s // tm),
        in_specs=[pl.BlockSpec((1, tm, d), tok),
                  pl.BlockSpec((1, 1, 8, d), lambda bi, i: (bi, jnp.minimum(i // ctx_tiles, 1), 0, 0)),
                  pl.BlockSpec((1, d), lambda bi, i: (0, 0)),
                  pl.BlockSpec((d, n_p), lambda bi, i: (0, 0))],
        out_specs=[pl.BlockSpec((1, tm, w), tok) for w in splits],
        out_shape=[jax.ShapeDtypeStruct((b, s, w), F32) for w in splits],
        compiler_params=_cparams("parallel", "arbitrary"),
        name="inproj",
    )(xall, mod, g, w_p)


def _rglru_kernel(rx_ref, rgx_ref, cw_ref, cb_ref, wg_ref, bg_ref, lam_ref, y_ref,
                  af, uf, ab, ub, *, ctx_len):
    s = rx_ref.shape[1]
    x = rx_ref[0]
    t = lax.broadcasted_iota(jnp.int32, (s, LANE), 0)
    is_lat = t >= ctx_len
    pos = jnp.where(is_lat, t - ctx_len, t)
    seg_len = jnp.where(is_lat, s - ctx_len, ctx_len)
    xm2 = jnp.where(pos >= 2, pltpu.roll(x, 2, 0), 0.0)
    xm1 = jnp.where(pos >= 1, pltpu.roll(x, 1, 0), 0.0)
    xp1 = jnp.where(pos < seg_len - 1, pltpu.roll(x, s - 1, 0), 0.0)
    cw = cw_ref[...]
    z = cw[0:1] * xm2 + cw[1:2] * xm1 + cw[2:3] * x + cw[3:4] * xp1 + cb_ref[...]
    pre = _dot(z, wg_ref[0]) + bg_ref[0]
    sp = _softplus(-lam_ref[...])
    for d, (a_ref, u_ref) in enumerate(((af, uf), (ab, ub))):
        r = _sigmoid(pre[:, (2 * d) * LANE:(2 * d + 1) * LANE])
        i = _sigmoid(pre[:, (2 * d + 1) * LANE:(2 * d + 2) * LANE])
        log_a = -RG_C * r * sp[d:d + 1]
        a_ref[...] = jnp.exp(log_a)
        u_ref[...] = jnp.sqrt(1.0 - jnp.exp(2.0 * log_a)) * (i * z)

    row = lax.broadcasted_iota(jnp.int32, (SUBLANE, LANE), 0)
    n_tiles = s // SUBLANE
    n_ctx = ctx_len // SUBLANE

    def body(n, carry):
        hf_prev, hb_prev = carry
        of = pl.multiple_of(n * SUBLANE, SUBLANE)
        a = af[pl.ds(of, SUBLANE), :]
        u = uf[pl.ds(of, SUBLANE), :]
        for sh in (1, 2, 4):
            a_sh = jnp.where(row >= sh, pltpu.roll(a, sh, 0), 1.0)
            u_sh = jnp.where(row >= sh, pltpu.roll(u, sh, 0), 0.0)
            u = a * u_sh + u
            a = a * a_sh
        h = a * hf_prev + u
        uf[pl.ds(of, SUBLANE), :] = h
        hf_new = h[SUBLANE - 1:SUBLANE, :]
        tb = jnp.where(n < n_ctx, n_ctx - 1 - n, n_tiles - 1 + n_ctx - n)
        ob = pl.multiple_of(tb * SUBLANE, SUBLANE)
        a = ab[pl.ds(ob, SUBLANE), :]
        u = ub[pl.ds(ob, SUBLANE), :]
        for sh in (1, 2, 4):
            a_sh = jnp.where(row < SUBLANE - sh, pltpu.roll(a, SUBLANE - sh, 0), 1.0)
            u_sh = jnp.where(row < SUBLANE - sh, pltpu.roll(u, SUBLANE - sh, 0), 0.0)
            u = a * u_sh + u
            a = a * a_sh
        h = a * hb_prev + u
        ub[pl.ds(ob, SUBLANE), :] = h
        return hf_new, h[0:1, :]

    zero = jnp.zeros((1, LANE), F32)
    lax.fori_loop(0, n_tiles, body, (zero, zero))
    y_ref[0] = (uf[...] + ub[...]) * _gelu_tanh(rgx_ref[0])


def _rglru(rg, conv_w, conv_b, wg, bg, lam, ctx_len):
    b, s, w2 = rg.shape
    nt = w2 // 2 // LANE
    return pl.pallas_call(
        functools.partial(_rglru_kernel, ctx_len=ctx_len),
        grid=(b, nt),
        in_specs=[pl.BlockSpec((1, s, LANE), lambda bi, j: (bi, 0, j)),
                  pl.BlockSpec((1, s, LANE), lambda bi, j: (bi, 0, nt + j)),
                  pl.BlockSpec((conv_w.shape[0], LANE), lambda bi, j: (0, j)),
                  pl.BlockSpec((1, LANE), lambda bi, j: (0, j)),
                  pl.BlockSpec((1, LANE, 4 * LANE), lambda bi, j: (j, 0, 0)),
                  pl.BlockSpec((1, 1, 4 * LANE), lambda bi, j: (j, 0, 0)),
                  pl.BlockSpec((2, LANE), lambda bi, j: (0, j))],
        out_specs=pl.BlockSpec((1, s, LANE), lambda bi, j: (bi, 0, j)),
        out_shape=jax.ShapeDtypeStruct((b, s, nt * LANE), F32),
        scratch_shapes=[pltpu.VMEM((s, LANE), F32)] * 4,
        compiler_params=_cparams("parallel", "arbitrary"),
        name="rglru",
    )(rg, rg, conv_w, conv_b, wg, bg, lam)


def _mlstm_kernel(q_ref, k_ref, v_ref, o_ref, gi_ref, gf_ref, gb_ref, ng_ref, out_ref,
                  packed, hf, hb, *, ctx_len, head_dim):
    s = q_ref.shape[1]
    ln = M_CHUNK
    n_chunks = s // ln
    n_ctx = ctx_len // ln
    n_local = q_ref.shape[2] // LANE
    head0 = pl.program_id(1) * n_local

    @pl.when(pl.program_id(1) == 0)
    def _():
        tmod = lax.broadcasted_iota(jnp.int32, (s, LANE), 0) % ln
        fwd = lax.broadcasted_iota(jnp.int32, (s, LANE), 1) < M_HEADS

        def scan(x, combine, fill):
            sh = 1
            while sh < ln:
                down = jnp.where(tmod >= sh, pltpu.roll(x, sh, 0), fill)
                up = jnp.where(tmod < ln - sh, pltpu.roll(x, s - sh, 0), fill)
                x = combine(x, jnp.where(fwd, down, up))
                sh *= 2
            return x

        bc = scan(-_softplus(-(gf_ref[0] + gb_ref[1:2, :])), jnp.add, 0.0)
        w = gi_ref[0] + gb_ref[0:1, :] - bc
        packed[0] = bc
        packed[1] = w
        packed[2] = scan(w, jnp.maximum, -jnp.inf)

    tt = lax.broadcasted_iota(jnp.int32, (ln, ln), 0)
    ss = lax.broadcasted_iota(jnp.int32, (ln, ln), 1)
    ones = jnp.ones((ln, LANE), MXU_DTYPE)
    k_scale = head_dim ** -0.5

    def chunk_step(c, j, state, rev):
        c_state, n_state, m_st = state
        o = pl.multiple_of(c * ln, ln)
        cols = slice(j * LANE, (j + 1) * LANE)
        pick = jnp.full((ln, LANE), rev * M_HEADS + j, jnp.int32) + head0
        bc, w, cm = [jnp.take_along_axis(packed[a, pl.ds(o, ln), :], pick, axis=1) for a in range(3)]
        w_row = jnp.transpose(w)
        mask = (ss >= tt) if rev else (ss <= tt)
        inter = bc + m_st
        m = jnp.maximum(bc + cm, inter)
        p = jnp.exp(jnp.where(mask, bc + w_row - m, -jnp.inf))
        q = q_ref[0, pl.ds(o, ln), cols].astype(MXU_DTYPE)
        kf = k_ref[0, pl.ds(o, ln), cols] * k_scale
        v = v_ref[0, pl.ds(o, ln), cols].astype(MXU_DTYPE)
        sc = lax.dot_general(q, kf.astype(MXU_DTYPE), (((1,), (1,)), ((), ())),
                             preferred_element_type=F32) * p
        sc_hi = sc.astype(MXU_DTYPE)
        sc_lo = (sc - sc_hi.astype(F32)).astype(MXU_DTYPE)
        e_inter = jnp.exp(inter - m)
        num = (jnp.dot(sc_hi, v, preferred_element_type=F32)
               + e_inter * jnp.dot(q, c_state.astype(MXU_DTYPE), preferred_element_type=F32))
        den = (jnp.dot(sc_hi, ones, preferred_element_type=F32)
               + jnp.dot(sc_lo, ones, preferred_element_type=F32)
               + e_inter * jnp.dot(q, n_state.astype(MXU_DTYPE), preferred_element_type=F32))
        h = num / jnp.maximum(jnp.abs(den), jnp.exp(-m))
        last = 0 if rev else ln - 1
        g = bc[last:last + 1, :]
        m_new = jnp.maximum(g + m_st, g + cm[last:last + 1, :])
        decay = jnp.exp(g + m_st - m_new)
        ks_t = jnp.transpose(kf * jnp.exp(g + w - m_new)).astype(MXU_DTYPE)
        c_new = decay * c_state + jnp.dot(ks_t, v, preferred_element_type=F32)
        n_new = decay * n_state + jnp.dot(ks_t, ones, preferred_element_type=F32)
        return h, (c_new, n_new, m_new)

    def body(n, carry):
        cr = jnp.where(n < n_ctx, n_ctx - 1 - n, n_chunks - 1 + n_ctx - n)
        new = []
        for j in range(n_local):
            cols = slice(j * LANE, (j + 1) * LANE)
            h, st_f = chunk_step(n, j, carry[2 * j], 0)
            hf[pl.ds(pl.multiple_of(n * ln, ln), ln), cols] = h
            h, st_b = chunk_step(cr, j, carry[2 * j + 1], 1)
            hb[pl.ds(pl.multiple_of(cr * ln, ln), ln), cols] = h
            new += [st_f, st_b]
        return tuple(new)

    zero = (jnp.zeros((LANE, LANE), F32), jnp.zeros((LANE, LANE), F32), jnp.zeros((1, LANE), F32))
    lax.fori_loop(0, n_chunks, body, (zero,) * (2 * n_local))

    for j in range(n_local):
        cols = slice(j * LANE, (j + 1) * LANE)
        hsum = hf[:, cols] + hb[:, cols]
        ms = jnp.sum(hsum * hsum, axis=1, keepdims=True) * (1.0 / head_dim)
        out_ref[0, :, cols] = _sigmoid(o_ref[0, :, cols]) * (hsum * lax.rsqrt(ms + EPS) * ng_ref[0, :, cols])


def _mlstm(qkvo, gates, gate_b, norm_g, ctx_len, head_dim):
    b, s, _ = qkvo.shape
    hps = 2
    steps = M_HEADS // hps
    w = hps * LANE
    blk = lambda off: pl.BlockSpec((1, s, w), lambda bi, hi: (bi, 0, off * steps + hi))
    return pl.pallas_call(
        functools.partial(_mlstm_kernel, ctx_len=ctx_len, head_dim=head_dim),
        grid=(b, steps),
        in_specs=[blk(0), blk(1), blk(2), blk(3),
                  pl.BlockSpec((1, s, LANE), lambda bi, hi: (bi, 0, 0)),
                  pl.BlockSpec((1, s, LANE), lambda bi, hi: (bi, 0, 1)),
                  pl.BlockSpec((2, LANE), lambda bi, hi: (0, 0)),
                  pl.BlockSpec((1, 1, w), lambda bi, hi: (hi, 0, 0))],
        out_specs=pl.BlockSpec((1, s, w), lambda bi, hi: (bi, 0, hi)),
        out_shape=jax.ShapeDtypeStruct((b, s, M_HEADS * LANE), F32),
        scratch_shapes=[pltpu.VMEM((3, s, LANE), F32), pltpu.VMEM((s, w), F32), pltpu.VMEM((s, w), F32)],
        compiler_params=_cparams("arbitrary", "arbitrary"),
        name="mlstm",
    )(qkvo, qkvo, qkvo, qkvo, gates, gates, gate_b, norm_g.reshape(steps, 1, w))


def _gmlp_kernel(cm_ref, ng_ref, ws_ref, bs_ref, y_ref, *, width):
    tm = cm_ref.shape[1]
    gd = width // CM_GROUPS
    lane = lax.broadcasted_iota(jnp.int32, (CM_CHUNK, width), 1)
    for c in range(tm // CM_CHUNK):
        rows = slice(c * CM_CHUNK, (c + 1) * CM_CHUNK)
        u = _gelu_tanh(cm_ref[0, rows, 0:width])
        v = _rms(_gelu_tanh(cm_ref[0, rows, width:2 * width]), ng_ref[...]).astype(MXU_DTYPE)
        mixed = bs_ref[...]
        for g in range(CM_GROUPS):
            mg = jnp.dot(ws_ref[g], v, preferred_element_type=F32)
            mixed = mixed + jnp.where((lane >= g * gd) & (lane < (g + 1) * gd), mg, 0.0)
        y_ref[0, rows, :] = u * mixed


def _gmlp(cm, norm_g, w_s, b_map):
    b, s, w2 = cm.shape
    width = w2 // 2
    tm = TOKEN_TILE
    return pl.pallas_call(
        functools.partial(_gmlp_kernel, width=width),
        grid=(b, s // tm),
        in_specs=[pl.BlockSpec((1, tm, w2), lambda bi, i: (bi, i, 0)),
                  pl.BlockSpec((1, width), lambda bi, i: (0, 0)),
                  pl.BlockSpec(w_s.shape, lambda bi, i: (0, 0, 0)),
                  pl.BlockSpec(b_map.shape, lambda bi, i: (0, 0))],
        out_specs=pl.BlockSpec((1, tm, width), lambda bi, i: (bi, i, 0)),
        out_shape=jax.ShapeDtypeStruct((b, s, width), F32),
        compiler_params=_cparams("parallel", "arbitrary"),
        name="gmlp",
    )(cm, norm_g, w_s, b_map)


def _outproj_router_kernel(yr_ref, ym_ref, yc_ref, x_ref, mod_ref, g_ref, wr_ref, wm_ref, wc_ref,
                           rw_ref, rb_ref, x1_ref, xb_ref, xrow_ref, idx_ref, wt_ref, rank_ref, cnt_ref,
                           counts):
    @pl.when((pl.program_id(0) == 0) & (pl.program_id(1) == 0))
    def _():
        counts[...] = jnp.zeros_like(counts)

    mix = (_dot(yr_ref[0], wr_ref[...]) + _dot(ym_ref[0], wm_ref[...]) + _dot(yc_ref[0], wc_ref[...]))
    x1 = x_ref[0] + mod_ref[0, 0, 2:3, :] * mix
    x1_ref[0] = x1
    xb = _rms(x1, g_ref[...]) * (1.0 + mod_ref[0, 0, 4:5, :]) + mod_ref[0, 0, 3:4, :]
    xb_ref[0] = xb.astype(xb_ref.dtype)
    for j in range(xb.shape[1] // LANE):
        xrow_ref[0, pl.ds(j, xb.shape[0], stride=SUBLANE), :] = xb[:, j * LANE:(j + 1) * LANE]

    logits = jnp.dot(xb, rw_ref[...], preferred_element_type=F32, precision=lax.Precision.HIGHEST)
    scores = _sigmoid(logits)
    sel = scores + rb_ref[...]
    tm, n_exp = sel.shape
    per_group = n_exp // N_GROUPS
    neg = -jnp.inf
    lane_h = lax.broadcasted_iota(jnp.int32, (tm, LANE), 1)
    lane_hf = lane_h.astype(F32)
    lane_e = lax.broadcasted_iota(jnp.int32, (tm, n_exp), 1)
    lane_ef = lane_e.astype(F32)
    groups_per_tile = LANE // per_group

    gscore = jnp.full((tm, LANE), neg, F32)
    for g in range(N_GROUPS):
        tile = sel[:, (g // groups_per_tile) * LANE:(g // groups_per_tile + 1) * LANE]
        mg = jnp.where(lane_h // per_group == g % groups_per_tile, tile, neg)
        m1 = jnp.max(mg, axis=1, keepdims=True)
        i1 = jnp.min(jnp.where(mg == m1, lane_hf, float(LANE)), axis=1, keepdims=True)
        m2 = jnp.max(jnp.where(lane_hf == i1, neg, mg), axis=1, keepdims=True)
        gscore = jnp.where(lane_h == g, m1 + m2, gscore)
    group_of = (lane_e // per_group).astype(F32)
    allowed = jnp.zeros((tm, n_exp), jnp.bool_)
    for _ in range(TOPK_GROUPS):
        m = jnp.max(gscore, axis=1, keepdims=True)
        gi = jnp.min(jnp.where(gscore == m, lane_hf, float(LANE)), axis=1, keepdims=True)
        gscore = jnp.where(lane_hf == gi, neg, gscore)
        allowed = allowed | (group_of == gi)
    cand = jnp.where(allowed, sel, neg)
    idx_out = jnp.zeros((tm, LANE), F32)
    w_out = jnp.zeros((tm, LANE), F32)
    chosen = jnp.zeros((tm, n_exp), F32)
    picks = []
    for k in range(TOP_K):
        m = jnp.max(cand, axis=1, keepdims=True)
        ik = jnp.min(jnp.where(cand == m, lane_ef, float(n_exp)), axis=1, keepdims=True)
        hit = lane_ef == ik
        wk = jnp.sum(jnp.where(hit, scores, 0.0), axis=1, keepdims=True)
        cand = jnp.where(hit, neg, cand)
        chosen = jnp.where(hit, 1.0, chosen)
        picks.append(ik)
        idx_out = jnp.where(lane_h == k, ik, idx_out)
        w_out = jnp.where(lane_h == k, wk, w_out)
    w_sum = jnp.sum(w_out, axis=1, keepdims=True)
    idx_ref[0] = idx_out.astype(jnp.int32)
    wt_ref[0] = w_out / w_sum * ROUTED_SCALE

    tt = lax.broadcasted_iota(jnp.int32, (tm, tm), 0)
    ss = lax.broadcasted_iota(jnp.int32, (tm, tm), 1)
    before = jnp.dot((ss < tt).astype(MXU_DTYPE), chosen.astype(MXU_DTYPE), preferred_element_type=F32)
    before = before + counts[...]
    rank_out = jnp.zeros((tm, LANE), F32)
    for k in range(TOP_K):
        rk = jnp.sum(jnp.where(lane_ef == picks[k], before, 0.0), axis=1, keepdims=True)
        rank_out = jnp.where(lane_h == k, rk, rank_out)
    rank_ref[0] = rank_out.astype(jnp.int32)
    counts[...] = counts[...] + jnp.sum(chosen, axis=0, keepdims=True)
    cnt_ref[...] = counts[...].astype(jnp.int32)


def _outproj_router(y_rg, y_m, y_cm, xall, mod, g2, w_r, w_m, w_c, router_w, router_b, ctx_tiles):
    b, s, d = xall.shape
    tm = TOKEN_TILE
    n_exp = router_w.shape[1]
    tok = lambda bi, i: (bi, i, 0)
    full2 = lambda bi, i: (0, 0)
    return pl.pallas_call(
        _outproj_router_kernel,
        grid=(b, s // tm),
        in_specs=[pl.BlockSpec((1, tm, y_rg.shape[2]), tok),
                  pl.BlockSpec((1, tm, y_m.shape[2]), tok),
                  pl.BlockSpec((1, tm, y_cm.shape[2]), tok),
                  pl.BlockSpec((1, tm, d), tok),
                  pl.BlockSpec((1, 1, 8, d), lambda bi, i: (bi, jnp.minimum(i // ctx_tiles, 1), 0, 0)),
                  pl.BlockSpec((1, d), full2),
                  pl.BlockSpec(w_r.shape, full2),
                  pl.BlockSpec(w_m.shape, full2),
                  pl.BlockSpec(w_c.shape, full2),
                  pl.BlockSpec(router_w.shape, full2),
                  pl.BlockSpec((1, router_w.shape[1]), full2)],
        out_specs=[pl.BlockSpec((1, tm, d), tok),
                   pl.BlockSpec((1, tm, d), tok),
                   pl.BlockSpec((1, tm * SUBLANE, LANE), tok),
                   pl.BlockSpec((1, tm, LANE), tok),
                   pl.BlockSpec((1, tm, LANE), tok),
                   pl.BlockSpec((1, tm, LANE), tok),
                   pl.BlockSpec((1, n_exp), full2)],
        out_shape=[jax.ShapeDtypeStruct((b, s, d), F32),
                   jax.ShapeDtypeStruct((b, s, d), MXU_DTYPE),
                   jax.ShapeDtypeStruct((b, s * SUBLANE, LANE), F32),
                   jax.ShapeDtypeStruct((b, s, LANE), jnp.int32),
                   jax.ShapeDtypeStruct((b, s, LANE), F32),
                   jax.ShapeDtypeStruct((b, s, LANE), jnp.int32),
                   jax.ShapeDtypeStruct((1, n_exp), jnp.int32)],
        scratch_shapes=[pltpu.VMEM((1, n_exp), F32)],
        compiler_params=_cparams("arbitrary", "arbitrary"),
        name="outproj_router",
    )(y_rg, y_m, y_cm, xall, mod, g2, w_r, w_m, w_c, router_w, router_b)


def _row_slab(ref, row):
    return ref.at[pl.ds(pl.multiple_of(row * SUBLANE, SUBLANE), SUBLANE), :]


def _dispatch_kernel(zb_ref, nz_ref, dest_ref, xrow_ref, xs_hbm, zbuf, sem, zsem):
    tm = xrow_ref.shape[1] // SUBLANE
    zrows = zbuf.shape[0]

    @pl.when((pl.program_id(0) == 0) & (pl.program_id(1) == 0))
    def _():
        zbuf[...] = jnp.zeros_like(zbuf)

        def z_start(n, carry):
            dst = xs_hbm.at[pl.ds(pl.multiple_of(zb_ref[n] * zrows, zrows), zrows), :]
            pltpu.make_async_copy(zbuf, dst, zsem).start()
            return carry

        def z_wait(n, carry):
            pltpu.make_async_copy(zbuf, xs_hbm.at[pl.ds(0, zrows), :], zsem).wait()
            return carry

        lax.fori_loop(0, nz_ref[0], z_start, 0)
        lax.fori_loop(0, nz_ref[0], z_wait, 0)

    def issue(r, carry):
        src = xrow_ref.at[0, pl.ds(pl.multiple_of(r * SUBLANE, SUBLANE), SUBLANE), :]
        for k in range(TOP_K):
            pltpu.make_async_copy(src, _row_slab(xs_hbm, dest_ref[0, 0, r * TOP_K + k]),
                                  sem).start(priority=k % 2)
        return carry

    lax.fori_loop(0, tm, issue, 0)
    n_all = tm * TOP_K * SUBLANE
    pltpu.make_async_copy(xs_hbm.at[pl.ds(0, n_all), :], xs_hbm.at[pl.ds(0, n_all), :], sem).wait()


def _dispatch(pad_blocks, n_pad_blocks, dest_tiles, xrow, n_rows):
    b, s8, _ = xrow.shape
    tm = TOKEN_TILE
    tiles = s8 // SUBLANE // tm
    grid_spec = pltpu.PrefetchScalarGridSpec(
        num_scalar_prefetch=2,
        grid=(b, tiles),
        in_specs=[pl.BlockSpec((1, 1, tm * TOP_K), lambda bi, i, zb, nz: (bi * tiles + i, 0, 0),
                               memory_space=pltpu.SMEM),
                  pl.BlockSpec((1, tm * SUBLANE, LANE), lambda bi, i, zb, nz: (bi, i, 0))],
        out_specs=pl.BlockSpec(memory_space=pl.ANY),
        scratch_shapes=[pltpu.VMEM((MOE_BLOCK * SUBLANE, LANE), F32),
                        pltpu.SemaphoreType.DMA(()), pltpu.SemaphoreType.DMA(())],
    )
    return pl.pallas_call(
        _dispatch_kernel,
        grid_spec=grid_spec,
        out_shape=jax.ShapeDtypeStruct((n_rows * SUBLANE, LANE), F32),
        compiler_params=_cparams("arbitrary", "arbitrary"),
        name="dispatch",
    )(pad_blocks, n_pad_blocks, dest_tiles, xrow)


def _moe_ffn_kernel(be_ref, nu_ref, xs_ref, wg_ref, wu_ref, wd_ref, out_ref, wgu, wdb):
    i = pl.program_id(0)
    bm = xs_ref.shape[0] // SUBLANE
    ff = wg_ref.shape[3]
    ffp = wdb.shape[0]
    first = (i == 0) | (be_ref[i] != be_ref[jnp.maximum(i - 1, 0)])

    @pl.when(i == 0)
    def _():
        wgu[...] = jnp.zeros_like(wgu)
        wdb[...] = jnp.zeros_like(wdb)

    @pl.when(first)
    def _():
        wgu[:, 0:ff] = wg_ref[0, 0].astype(MXU_DTYPE)
        wgu[:, ffp:ffp + ff] = wu_ref[0, 0].astype(MXU_DTYPE)
        wdb[0:ff, :] = wd_ref[0, 0].astype(MXU_DTYPE)

    @pl.when(i < nu_ref[0])
    def _():
        x = jnp.concatenate([xs_ref[pl.ds(j, bm, stride=SUBLANE), :] for j in range(SUBLANE)],
                            axis=1).astype(MXU_DTYPE)
        r = jnp.dot(x, wgu[...], preferred_element_type=F32)
        hg = r[:, 0:ffp]
        h = (hg * _sigmoid(hg) * r[:, ffp:2 * ffp]).astype(MXU_DTYPE)
        y = jnp.dot(h, wdb[...], preferred_element_type=F32)
        for j in range(SUBLANE):
            out_ref[pl.ds(j, bm, stride=SUBLANE), :] = y[:, j * LANE:(j + 1) * LANE]

    @pl.when(i >= nu_ref[0])
    def _():
        out_ref[...] = jnp.zeros_like(out_ref)


def _moe_ffn(block_e, n_used, xs, e_gate, e_up, e_down, layer):
    bm = MOE_BLOCK
    n_blocks = xs.shape[0] // SUBLANE // bm
    _, _, d, ff = e_gate.shape
    ffp = -(-ff // LANE) * LANE
    used = lambda i, be, nu: (jnp.minimum(i, nu[0] - 1), 0)
    expert = lambda i, be, nu: (layer, be[i], 0, 0)
    grid_spec = pltpu.PrefetchScalarGridSpec(
        num_scalar_prefetch=2,
        grid=(n_blocks,),
        in_specs=[pl.BlockSpec((bm * SUBLANE, LANE), used),
                  pl.BlockSpec((1, 1, d, ff), expert),
                  pl.BlockSpec((1, 1, d, ff), expert),
                  pl.BlockSpec((1, 1, ff, d), expert)],
        out_specs=pl.BlockSpec((bm * SUBLANE, LANE), lambda i, be, nu: (i, 0)),
        scratch_shapes=[pltpu.VMEM((d, 2 * ffp), MXU_DTYPE), pltpu.VMEM((ffp, d), MXU_DTYPE)],
    )
    return pl.pallas_call(
        _moe_ffn_kernel,
        grid_spec=grid_spec,
        out_shape=jax.ShapeDtypeStruct(xs.shape, F32),
        compiler_params=_cparams("arbitrary"),
        name="moe_ffn",
    )(block_e, n_used, xs, e_gate, e_up, e_down)


def _ffn_out_kernel(dest_ref, x1_ref, xb_ref, wt_ref, mod_ref, sg_ref, su_ref, sd_ref, fg_ref, ys_hbm,
                    out_ref, buf, sem, *, final):
    tm = x1_ref.shape[1]

    def issue(r, carry):
        for k in range(TOP_K):
            pltpu.make_async_copy(_row_slab(ys_hbm, dest_ref[0, 0, r * TOP_K + k]),
                                  _row_slab(buf, k * tm + r), sem).start(priority=k % 2)
        return carry

    lax.fori_loop(0, tm, issue, 0)
    xb = xb_ref[0]
    hg = jnp.dot(xb, sg_ref[...], preferred_element_type=F32)
    hu = jnp.dot(xb, su_ref[...], preferred_element_type=F32)
    y_sh = _dot(hg * _sigmoid(hg) * hu, sd_ref[...])
    pltpu.make_async_copy(ys_hbm.at[pl.ds(0, buf.shape[0]), :], buf, sem).wait()
    w = wt_ref[0]
    cols = []
    for j in range(SUBLANE):
        acc = w[:, 0:1] * buf[pl.ds(j, tm, stride=SUBLANE), :]
        for k in range(1, TOP_K):
            acc = acc + w[:, k:k + 1] * buf[pl.ds(k * tm * SUBLANE + j, tm, stride=SUBLANE), :]
        cols.append(acc)
    x2 = x1_ref[0] + mod_ref[0, 0, 5:6, :] * (jnp.concatenate(cols, axis=1) + y_sh)
    if final:
        x2 = _rms(x2, fg_ref[...])
    out_ref[0] = x2


def _ffn_out(dest_tiles, x1, xb, wts, ys, mod, sg, su, sd, final_g, ctx_tiles, final):
    b, s, d = x1.shape
    tm = TOKEN_TILE
    tiles = s // tm
    skip = ctx_tiles if final else 0
    n_tiles = tiles - skip
    tok = lambda bi, i: (bi, i + skip, 0)
    full2 = lambda bi, i: (0, 0)
    return pl.pallas_call(
        functools.partial(_ffn_out_kernel, final=final),
        grid=(b, n_tiles),
        in_specs=[pl.BlockSpec((1, 1, tm * TOP_K), lambda bi, i: (bi * tiles + i + skip, 0, 0),
                               memory_space=pltpu.SMEM),
                  pl.BlockSpec((1, tm, d), tok),
                  pl.BlockSpec((1, tm, d), tok),
                  pl.BlockSpec((1, tm, LANE), tok),
                  pl.BlockSpec((1, 1, 8, d),
                               lambda bi, i: (bi, jnp.minimum((i + skip) // ctx_tiles, 1), 0, 0)),
                  pl.BlockSpec(sg.shape, full2),
                  pl.BlockSpec(su.shape, full2),
                  pl.BlockSpec(sd.shape, full2),
                  pl.BlockSpec((1, d), full2),
                  pl.BlockSpec(memory_space=pl.ANY)],
        out_specs=pl.BlockSpec((1, tm, d), lambda bi, i: (bi, i, 0)),
        out_shape=jax.ShapeDtypeStruct((b, n_tiles * tm, d), F32),
        scratch_shapes=[pltpu.VMEM((TOP_K * tm * SUBLANE, LANE), F32), pltpu.SemaphoreType.DMA(())],
        compiler_params=_cparams("parallel", "arbitrary"),
        name="ffn_out",
    )(dest_tiles, x1, xb, wts, mod, sg, su, sd, final_g, ys)


def _pos_embed_2d(rows, dim):
    quarter = dim // 4
    omega = 1.0 / (10000.0 ** (jnp.arange(quarter, dtype=F32) / quarter))
    row = jnp.repeat(jnp.arange(rows, dtype=F32), GRID_W)
    col = jnp.tile(jnp.arange(GRID_W, dtype=F32), rows)

    def axis_embed(p):
        ang = p[:, None] * omega[None, :]
        return jnp.concatenate([jnp.sin(ang), jnp.cos(ang)], axis=-1)

    return jnp.concatenate([axis_embed(row), axis_embed(col)], axis=-1)


def _pad_heads(w, n_heads, axis):
    shape = w.shape
    hd = shape[axis] // n_heads
    w = w.reshape(shape[:axis] + (n_heads, hd) + shape[axis + 1:])
    pad = [(0, 0)] * w.ndim
    pad[axis + 1] = (0, LANE - hd)
    w = jnp.pad(w, pad)
    return w.reshape(shape[:axis] + (n_heads * LANE,) + shape[axis + 1:])


def _routing_tables(eidx, rank, counts, bm):
    t, k = eidx.shape
    n_exp = counts.shape[0]
    padded = (counts + bm - 1) // bm * bm
    pad_end = jnp.cumsum(padded)
    pad_start = pad_end - padded
    n_blocks = -(-(t * k + n_exp * (bm - 1)) // bm)
    expert_id = jnp.arange(n_exp, dtype=eidx.dtype)
    dest = jnp.sum(jnp.where(eidx[:, :, None] == expert_id, pad_start, 0), axis=-1) + rank
    block_start = jnp.arange(n_blocks, dtype=jnp.int32) * bm
    block_e = jnp.minimum(jnp.sum(pad_end[None, :] <= block_start[:, None], axis=1), n_exp - 1)
    n_used = (pad_end[-1] // bm).reshape(1)
    has_pad = ((pad_end[block_e] == block_start + bm) & (counts[block_e] % bm != 0)) | (block_start >= pad_end[-1])
    slot = jnp.cumsum(has_pad) - 1
    block_id = jnp.arange(n_blocks, dtype=jnp.int32)
    pad_blocks = jnp.sum(jnp.where(has_pad[None, :] & (slot[None, :] == block_id[:, None]), block_id[None, :], 0),
                         axis=1)
    n_pad_blocks = jnp.sum(has_pad).reshape(1)
    return (dest.astype(jnp.int32), block_e.astype(jnp.int32), n_used.astype(jnp.int32),
            pad_blocks.astype(jnp.int32), n_pad_blocks.astype(jnp.int32), n_blocks * bm)


def kernel(x, c, ctx, c_ctx, norm1_g, norm2_g, w_mod, b_mod, w_in, rg_conv_w, rg_conv_b, rg_gate_w, rg_gate_b, rg_lambda, mlstm_gate_b, mlstm_norm_g, cm_norm_g, cm_w, cm_b, w_out, router_w, router_b, exp_gate, exp_up, exp_down, sh_gate, sh_up, sh_down, final_norm_g):
    batch, seq, d = x.shape
    ctx_len = ctx.shape[1]
    depth = w_in.shape[0]
    rg_w = rg_conv_w.shape[2]
    m_w = mlstm_norm_g.shape[1]
    cm_width = cm_norm_g.shape[1]
    head_dim = m_w // M_HEADS
    rg_hd = rg_w // RG_HEADS
    n_exp = router_w.shape[2]
    n_gates = 4 * M_HEADS
    assert ctx_len % TOKEN_TILE == 0 and seq % TOKEN_TILE == 0
    assert rg_w % LANE == 0 and LANE % rg_hd == 0 and head_dim < LANE and cm_width % LANE == 0
    assert d == SUBLANE * LANE
    ctx_tiles = ctx_len // TOKEN_TILE
    s_all = ctx_len + seq
    n_tok = batch * s_all

    xall = jnp.concatenate([ctx, x + _pos_embed_2d(seq // GRID_W, d).astype(x.dtype)[None]], axis=1)
    cond = jnp.concatenate([c, c_ctx[None], jnp.zeros((SUBLANE - (batch + 1) % SUBLANE, d), F32)], axis=0)

    out = None
    for l in range(depth):
        last = l == depth - 1
        sizes = (rg_w, rg_w, m_w, m_w, m_w, m_w, n_gates, 2 * cm_width)
        offs = np.cumsum((0,) + sizes)
        w_cols = [w_in[l][:, offs[i]:offs[i + 1]] for i in range(len(sizes))]
        w_gate = w_cols[6].reshape(d, 2, 2, M_HEADS)
        lane_pad = ((0, 0), (0, LANE - 2 * M_HEADS))
        w_p = jnp.concatenate(
            [w_cols[0], w_cols[1]] + [_pad_heads(w_cols[i], M_HEADS, 1) for i in (2, 3, 4, 5)]
            + [jnp.pad(w_gate[:, :, g, :].reshape(d, 2 * M_HEADS), lane_pad) for g in range(2)]
            + [w_cols[7]], axis=1).astype(MXU_DTYPE)
        splits = (2 * rg_w, 4 * M_HEADS * LANE, 2 * LANE, 2 * cm_width)
        hp = LANE // rg_hd
        gw = rg_gate_w[l].reshape(4, RG_HEADS // hp, hp, rg_hd, rg_hd)
        eye = jnp.eye(hp, dtype=F32)
        wg = jnp.einsum('gthij,hk->tghikj', gw, eye).reshape(RG_HEADS // hp, 4, LANE, LANE)
        wg = wg.transpose(0, 2, 1, 3).reshape(RG_HEADS // hp, LANE, 4 * LANE).astype(MXU_DTYPE)
        bg = rg_gate_b[l].reshape(4, RG_HEADS // hp, LANE).transpose(1, 0, 2).reshape(RG_HEADS // hp, 1, 4 * LANE)
        gate_b = jnp.pad(mlstm_gate_b[l].transpose(1, 0, 2).reshape(2, 2 * M_HEADS), lane_pad)
        m_norm_g = _pad_heads(mlstm_norm_g[l], M_HEADS, 0).reshape(M_HEADS, 1, LANE)
        cm_ws = cm_w[l].astype(MXU_DTYPE)
        cm_bmap = jnp.repeat(cm_b[l].T, cm_width // CM_GROUPS, axis=1)
        w_o = w_out[l]
        w_r = w_o[:rg_w].astype(MXU_DTYPE)
        w_m = _pad_heads(w_o[rg_w:rg_w + m_w], M_HEADS, 0).astype(MXU_DTYPE)
        w_c = w_o[rg_w + m_w:].astype(MXU_DTYPE)

        mod = _adaln(cond, w_mod, b_mod, l).reshape(cond.shape[0], 6, d)
        mod = jnp.pad(mod, ((0, 0), (0, 2), (0, 0)))
        mod = jnp.stack([jnp.broadcast_to(mod[batch], (batch, 8, d)), mod[:batch]], axis=1)

        rg, qkvo, gates, cm = _inproj(xall, mod, norm1_g[l].reshape(1, d), w_p, splits, ctx_tiles)
        y_rg = _rglru(rg, rg_conv_w[l], rg_conv_b[l].reshape(1, rg_w), wg, bg, rg_lambda[l], ctx_len)
        y_m = _mlstm(qkvo, gates, gate_b, m_norm_g, ctx_len, head_dim)
        y_cm = _gmlp(cm, cm_norm_g[l].reshape(1, cm_width), cm_ws, cm_bmap)
        x1, xb, xrow, eidx, wts, rank, counts = _outproj_router(
            y_rg, y_m, y_cm, xall, mod, norm2_g[l].reshape(1, d), w_r, w_m, w_c, router_w[l],
            router_b[l].reshape(1, n_exp), ctx_tiles)

        dest, block_e, n_used, pad_blocks, n_pad_blocks, n_rows = _routing_tables(
            eidx.reshape(n_tok, LANE)[:, :TOP_K], rank.reshape(n_tok, LANE)[:, :TOP_K], counts[0], MOE_BLOCK)
        dest_tiles = dest.reshape(n_tok // TOKEN_TILE, 1, TOKEN_TILE * TOP_K)
        xs = _dispatch(pad_blocks, n_pad_blocks, dest_tiles, xrow, n_rows)
        ys = _moe_ffn(block_e, n_used, xs, exp_gate, exp_up, exp_down, l)
        out = _ffn_out(dest_tiles, x1, xb, wts, ys, mod, sh_gate[l].astype(MXU_DTYPE),
                       sh_up[l].astype(MXU_DTYPE), sh_down[l].astype(MXU_DTYPE),
                       final_norm_g.reshape(1, d), ctx_tiles, last)
        xall = out
    return out
```

```python
import functools

import numpy as np
import jax
import jax.numpy as jnp
from jax import lax
from jax.experimental import pallas as pl
from jax.experimental.pallas import tpu as pltpu

F32 = jnp.float32
MXU_DTYPE = jnp.bfloat16
LANE = 128
SUBLANE = 8
VMEM_LIMIT = 56 * 1024 * 1024

GRID_W = 64
RG_HEADS = 6
RG_C = 8.0
M_HEADS = 4
M_CHUNK = 128
CM_GROUPS = 4
CM_CHUNK = 128
N_GROUPS = 8
TOPK_GROUPS = 4
TOP_K = 8
ROUTED_SCALE = 2.5
EPS = 1e-6

ROW_DTYPE = jnp.float32
TOK_ROWS = 8
TOKEN_TILE = 256
ROUTE_ROWS = 128
MOE_BLOCK = 256


def _cparams(*sem):
    return pltpu.CompilerParams(dimension_semantics=sem, vmem_limit_bytes=VMEM_LIMIT)


def _dot(a, b):
    return jnp.dot(a.astype(MXU_DTYPE), b.astype(MXU_DTYPE), preferred_element_type=F32)


def _sigmoid(x):
    return 1.0 / (1.0 + jnp.exp(-x))


def _gelu_tanh(x):
    return 0.5 * x * (1.0 + jnp.tanh(0.7978845608028654 * (x + 0.044715 * (x * x * x))))


def _softplus(x):
    return jnp.maximum(x, 0.0) + jnp.log(1.0 + jnp.exp(-jnp.abs(x)))


def _rms(x, g):
    return x * lax.rsqrt(jnp.mean(x * x, axis=-1, keepdims=True) + EPS) * g


def _pack_rows(x):
    return [x[:, j * LANE:(j + 1) * LANE] for j in range(TOK_ROWS)]


def _unpack_rows(rows):
    return jnp.concatenate(rows, axis=1)


def _adaln_kernel(c_ref, w_ref, b_ref, o_ref):
    c = c_ref[...]
    o_ref[...] = _dot(c * _sigmoid(c), w_ref[0]) + b_ref[0]


def _adaln(cond, w_mod, b_mod, layer):
    n, d = cond.shape
    depth, _, n_out = w_mod.shape
    tn = 1536
    return pl.pallas_call(
        _adaln_kernel,
        grid=(n_out // tn,),
        in_specs=[pl.BlockSpec((n, d), lambda j: (0, 0)),
                  pl.BlockSpec((1, d, tn), lambda j: (layer, 0, j)),
                  pl.BlockSpec((1, 1, tn), lambda j: (layer, 0, j))],
        out_specs=pl.BlockSpec((n, tn), lambda j: (0, j)),
        out_shape=jax.ShapeDtypeStruct((n, n_out), F32),
        compiler_params=_cparams("arbitrary"),
        name="adaln",
    )(cond, w_mod, b_mod.reshape(depth, 1, n_out))


def _inproj_kernel(x_ref, mod_ref, g_ref, w_ref, rg_ref, qkvo_ref, gt_ref, cm_ref, *, splits):
    x = x_ref[0]
    shift = mod_ref[0, 0, 0:1, :]
    scale = mod_ref[0, 0, 1:2, :]
    xa = (_rms(x, g_ref[...]) * (1.0 + scale) + shift).astype(MXU_DTYPE)
    a = 0
    for ref, width in zip((rg_ref, qkvo_ref, gt_ref, cm_ref), splits):
        ref[0] = jnp.dot(xa, w_ref[:, a:a + width], preferred_element_type=F32)
        a += width


def _inproj(xall, mod, g, w_p, splits, ctx_tiles):
    b, s, d = xall.shape
    tm = TOKEN_TILE
    n_p = w_p.shape[1]
    tok = lambda bi, i: (bi, i, 0)
    return pl.pallas_call(
        functools.partial(_inproj_kernel, splits=splits),
        grid=(b, s // tm),
        in_specs=[pl.BlockSpec((1, tm, d), tok),
                  pl.BlockSpec((1, 1, 8, d), lambda bi, i: (bi, jnp.minimum(i // ctx_tiles, 1), 0, 0)),
                  pl.BlockSpec((1, d), lambda bi, i: (0, 0)),
                  pl.BlockSpec((d, n_p), lambda bi, i: (0, 0))],
        out_specs=[pl.BlockSpec((1, tm, w), tok) for w in splits],
        out_shape=[jax.ShapeDtypeStruct((b, s, w), F32) for w in splits],
        compiler_params=_cparams("parallel", "arbitrary"),
        name="inproj",
    )(xall, mod, g, w_p)


def _rglru_kernel(rx_ref, rgx_ref, cw_ref, cb_ref, wg_ref, bg_ref, lam_ref, y_ref,
                  af, uf, ab, ub, *, ctx_len):
    s = rx_ref.shape[1]
    x = rx_ref[0]
    t = lax.broadcasted_iota(jnp.int32, (s, LANE), 0)
    is_lat = t >= ctx_len
    pos = jnp.where(is_lat, t - ctx_len, t)
    seg_len = jnp.where(is_lat, s - ctx_len, ctx_len)
    xm2 = jnp.where(pos >= 2, pltpu.roll(x, 2, 0), 0.0)
    xm1 = jnp.where(pos >= 1, pltpu.roll(x, 1, 0), 0.0)
    xp1 = jnp.where(pos < seg_len - 1, pltpu.roll(x, s - 1, 0), 0.0)
    cw = cw_ref[...]
    z = cw[0:1] * xm2 + cw[1:2] * xm1 + cw[2:3] * x + cw[3:4] * xp1 + cb_ref[...]
    pre = _dot(z, wg_ref[0]) + bg_ref[0]
    sp = _softplus(-lam_ref[...])
    for d, (a_ref, u_ref) in enumerate(((af, uf), (ab, ub))):
        r = _sigmoid(pre[:, (2 * d) * LANE:(2 * d + 1) * LANE])
        i = _sigmoid(pre[:, (2 * d + 1) * LANE:(2 * d + 2) * LANE])
        log_a = -RG_C * r * sp[d:d + 1]
        a_ref[...] = jnp.exp(log_a)
        u_ref[...] = jnp.sqrt(1.0 - jnp.exp(2.0 * log_a)) * (i * z)

    row = lax.broadcasted_iota(jnp.int32, (SUBLANE, LANE), 0)
    n_tiles = s // SUBLANE
    n_ctx = ctx_len // SUBLANE

    def body(n, carry):
        hf_prev, hb_prev = carry
        of = pl.multiple_of(n * SUBLANE, SUBLANE)
        a = af[pl.ds(of, SUBLANE), :]
        u = uf[pl.ds(of, SUBLANE), :]
        for sh in (1, 2, 4):
            a_sh = jnp.where(row >= sh, pltpu.roll(a, sh, 0), 1.0)
            u_sh = jnp.where(row >= sh, pltpu.roll(u, sh, 0), 0.0)
            u = a * u_sh + u
            a = a * a_sh
        h = a * hf_prev + u
        uf[pl.ds(of, SUBLANE), :] = h
        hf_new = h[SUBLANE - 1:SUBLANE, :]
        tb = jnp.where(n < n_ctx, n_ctx - 1 - n, n_tiles - 1 + n_ctx - n)
        ob = pl.multiple_of(tb * SUBLANE, SUBLANE)
        a = ab[pl.ds(ob, SUBLANE), :]
        u = ub[pl.ds(ob, SUBLANE), :]
        for sh in (1, 2, 4):
            a_sh = jnp.where(row < SUBLANE - sh, pltpu.roll(a, SUBLANE - sh, 0), 1.0)
            u_sh = jnp.where(row < SUBLANE - sh, pltpu.roll(u, SUBLANE - sh, 0), 0.0)
            u = a * u_sh + u
            a = a * a_sh
        h = a * hb_prev + u
        ub[pl.ds(ob, SUBLANE), :] = h
        return hf_new, h[0:1, :]

    zero = jnp.zeros((1, LANE), F32)
    lax.fori_loop(0, n_tiles, body, (zero, zero))
    y_ref[0] = (uf[...] + ub[...]) * _gelu_tanh(rgx_ref[0])


def _rglru(rg, conv_w, conv_b, wg, bg, lam, ctx_len):
    b, s, w2 = rg.shape
    nt = w2 // 2 // LANE
    return pl.pallas_call(
        functools.partial(_rglru_kernel, ctx_len=ctx_len),
        grid=(b, nt),
        in_specs=[pl.BlockSpec((1, s, LANE), lambda bi, j: (bi, 0, j)),
                  pl.BlockSpec((1, s, LANE), lambda bi, j: (bi, 0, nt + j)),
                  pl.BlockSpec((conv_w.shape[0], LANE), lambda bi, j: (0, j)),
                  pl.BlockSpec((1, LANE), lambda bi, j: (0, j)),
                  pl.BlockSpec((1, LANE, 4 * LANE), lambda bi, j: (j, 0, 0)),
                  pl.BlockSpec((1, 1, 4 * LANE), lambda bi, j: (j, 0, 0)),
                  pl.BlockSpec((2, LANE), lambda bi, j: (0, j))],
        out_specs=pl.BlockSpec((1, s, LANE), lambda bi, j: (bi, 0, j)),
        out_shape=jax.ShapeDtypeStruct((b, s, nt * LANE), F32),
        scratch_shapes=[pltpu.VMEM((s, LANE), F32)] * 4,
        compiler_params=_cparams("parallel", "arbitrary"),
        name="rglru",
    )(rg, rg, conv_w, conv_b, wg, bg, lam)


def _mlstm_kernel(q_ref, k_ref, v_ref, o_ref, gi_ref, gf_ref, gb_ref, ng_ref, out_ref,
                  packed, hf, hb, *, ctx_len, head_dim):
    s = q_ref.shape[1]
    ln = M_CHUNK
    n_chunks = s // ln
    n_ctx = ctx_len // ln
    n_local = q_ref.shape[2] // LANE
    head0 = pl.program_id(1) * n_local

    @pl.when(pl.program_id(1) == 0)
    def _():
        tmod = lax.broadcasted_iota(jnp.int32, (s, LANE), 0) % ln
        fwd = lax.broadcasted_iota(jnp.int32, (s, LANE), 1) < M_HEADS

        def scan(x, combine, fill):
            sh = 1
            while sh < ln:
                down = jnp.where(tmod >= sh, pltpu.roll(x, sh, 0), fill)
                up = jnp.where(tmod < ln - sh, pltpu.roll(x, s - sh, 0), fill)
                x = combine(x, jnp.where(fwd, down, up))
                sh *= 2
            return x

        bc = scan(-_softplus(-(gf_ref[0] + gb_ref[1:2, :])), jnp.add, 0.0)
        w = gi_ref[0] + gb_ref[0:1, :] - bc
        packed[0] = bc
        packed[1] = w
        packed[2] = scan(w, jnp.maximum, -jnp.inf)

    tt = lax.broadcasted_iota(jnp.int32, (ln, ln), 0)
    ss = lax.broadcasted_iota(jnp.int32, (ln, ln), 1)
    ones = jnp.ones((ln, LANE), MXU_DTYPE)
    k_scale = head_dim ** -0.5

    def chunk_step(c, j, state, rev):
        c_state, n_state, m_st = state
        o = pl.multiple_of(c * ln, ln)
        cols = slice(j * LANE, (j + 1) * LANE)
        pick = jnp.full((ln, LANE), rev * M_HEADS + j, jnp.int32) + head0
        bc, w, cm = [jnp.take_along_axis(packed[a, pl.ds(o, ln), :], pick, axis=1) for a in range(3)]
        w_row = jnp.transpose(w)
        mask = (ss >= tt) if rev else (ss <= tt)
        inter = bc + m_st
        m = jnp.maximum(bc + cm, inter)
        p = jnp.exp(jnp.where(mask, bc + w_row - m, -jnp.inf))
        q = q_ref[0, pl.ds(o, ln), cols].astype(MXU_DTYPE)
        kf = k_ref[0, pl.ds(o, ln), cols] * k_scale
        v = v_ref[0, pl.ds(o, ln), cols].astype(MXU_DTYPE)
        sc = lax.dot_general(q, kf.astype(MXU_DTYPE), (((1,), (1,)), ((), ())),
                             preferred_element_type=F32) * p
        sc_hi = sc.astype(MXU_DTYPE)
        sc_lo = (sc - sc_hi.astype(F32)).astype(MXU_DTYPE)
        e_inter = jnp.exp(inter - m)
        num = (jnp.dot(sc_hi, v, preferred_element_type=F32)
               + e_inter * jnp.dot(q, c_state.astype(MXU_DTYPE), preferred_element_type=F32))
        den = (jnp.dot(sc_hi, ones, preferred_element_type=F32)
               + jnp.dot(sc_lo, ones, preferred_element_type=F32)
               + e_inter * jnp.dot(q, n_state.astype(MXU_DTYPE), preferred_element_type=F32))
        h = num / jnp.maximum(jnp.abs(den), jnp.exp(-m))
        last = 0 if rev else ln - 1
        g = bc[last:last + 1, :]
        m_new = jnp.maximum(g + m_st, g + cm[last:last + 1, :])
        decay = jnp.exp(g + m_st - m_new)
        ks_t = jnp.transpose(kf * jnp.exp(g + w - m_new)).astype(MXU_DTYPE)
        c_new = decay * c_state + jnp.dot(ks_t, v, preferred_element_type=F32)
        n_new = decay * n_state + jnp.dot(ks_t, ones, preferred_element_type=F32)
        return h, (c_new, n_new, m_new)

    def body(n, carry):
        cr = jnp.where(n < n_ctx, n_ctx - 1 - n, n_chunks - 1 + n_ctx - n)
        new = []
        for j in range(n_local):
            cols = slice(j * LANE, (j + 1) * LANE)
            h, st_f = chunk_step(n, j, carry[2 * j], 0)
            hf[pl.ds(pl.multiple_of(n * ln, ln), ln), cols] = h
            h, st_b = chunk_step(cr, j, carry[2 * j + 1], 1)
            hb[pl.ds(pl.multiple_of(cr * ln, ln), ln), cols] = h
            new += [st_f, st_b]
        return tuple(new)

    zero = (jnp.zeros((LANE, LANE), F32), jnp.zeros((LANE, LANE), F32), jnp.zeros((1, LANE), F32))
    lax.fori_loop(0, n_chunks, body, (zero,) * (2 * n_local))

    for j in range(n_local):
        cols = slice(j * LANE, (j + 1) * LANE)
        hsum = hf[:, cols] + hb[:, cols]
        ms = jnp.sum(hsum * hsum, axis=1, keepdims=True) * (1.0 / head_dim)
        out_ref[0, :, cols] = _sigmoid(o_ref[0, :, cols]) * (hsum * lax.rsqrt(ms + EPS) * ng_ref[0, :, cols])


def _mlstm(qkvo, gates, gate_b, norm_g, ctx_len, head_dim):
    b, s, _ = qkvo.shape
    hps = 2
    steps = M_HEADS // hps
    w = hps * LANE
    blk = lambda off: pl.BlockSpec((1, s, w), lambda bi, hi: (bi, 0, off * steps + hi))
    return pl.pallas_call(
        functools.partial(_mlstm_kernel, ctx_len=ctx_len, head_dim=head_dim),
        grid=(b, steps),
        in_specs=[blk(0), blk(1), blk(2), blk(3),
                  pl.BlockSpec((1, s, LANE), lambda bi, hi: (bi, 0, 0)),
                  pl.BlockSpec((1, s, LANE), lambda bi, hi: (bi, 0, 1)),
                  pl.BlockSpec((2, LANE), lambda bi, hi: (0, 0)),
                  pl.BlockSpec((1, 1, w), lambda bi, hi: (hi, 0, 0))],
        out_specs=pl.BlockSpec((1, s, w), lambda bi, hi: (bi, 0, hi)),
        out_shape=jax.ShapeDtypeStruct((b, s, M_HEADS * LANE), F32),
        scratch_shapes=[pltpu.VMEM((3, s, LANE), F32), pltpu.VMEM((s, w), F32), pltpu.VMEM((s, w), F32)],
        compiler_params=_cparams("arbitrary", "arbitrary"),
        name="mlstm",
    )(qkvo, qkvo, qkvo, qkvo, gates, gates, gate_b, norm_g.reshape(steps, 1, w))


def _gmlp_kernel(cm_ref, ng_ref, ws_ref, bs_ref, y_ref, *, width):
    tm = cm_ref.shape[1]
    gd = width // CM_GROUPS
    lane = lax.broadcasted_iota(jnp.int32, (CM_CHUNK, width), 1)
    for c in range(tm // CM_CHUNK):
        rows = slice(c * CM_CHUNK, (c + 1) * CM_CHUNK)
        u = _gelu_tanh(cm_ref[0, rows, 0:width])
        v = _rms(_gelu_tanh(cm_ref[0, rows, width:2 * width]), ng_ref[...]).astype(MXU_DTYPE)
        mixed = bs_ref[...]
        for g in range(CM_GROUPS):
            mg = jnp.dot(ws_ref[g], v, preferred_element_type=F32)
            mixed = mixed + jnp.where((lane >= g * gd) & (lane < (g + 1) * gd), mg, 0.0)
        y_ref[0, rows, :] = u * mixed


def _gmlp(cm, norm_g, w_s, b_map):
    b, s, w2 = cm.shape
    width = w2 // 2
    tm = TOKEN_TILE
    return pl.pallas_call(
        functools.partial(_gmlp_kernel, width=width),
        grid=(b, s // tm),
        in_specs=[pl.BlockSpec((1, tm, w2), lambda bi, i: (bi, i, 0)),
                  pl.BlockSpec((1, width), lambda bi, i: (0, 0)),
                  pl.BlockSpec(w_s.shape, lambda bi, i: (0, 0, 0)),
                  pl.BlockSpec(b_map.shape, lambda bi, i: (0, 0))],
        out_specs=pl.BlockSpec((1, tm, width), lambda bi, i: (bi, i, 0)),
        out_shape=jax.ShapeDtypeStruct((b, s, width), F32),
        compiler_params=_cparams("parallel", "arbitrary"),
        name="gmlp",
    )(cm, norm_g, w_s, b_map)


def _outproj_router_kernel(yr_ref, ym_ref, yc_ref, x_ref, mod_ref, g_ref, wr_ref, wm_ref, wc_ref,
                           rw_ref, rb_ref, x1_ref, xb_ref, xrow_ref, idx_ref, wt_ref, rank_ref, cnt_ref,
                           counts):
    @pl.when((pl.program_id(0) == 0) & (pl.program_id(1) == 0))
    def _():
        counts[...] = jnp.zeros_like(counts)

    mix = (_dot(yr_ref[0], wr_ref[...]) + _dot(ym_ref[0], wm_ref[...]) + _dot(yc_ref[0], wc_ref[...]))
    x1 = x_ref[0] + mod_ref[0, 0, 2:3, :] * mix
    x1_ref[0] = x1
    xb = _rms(x1, g_ref[...]) * (1.0 + mod_ref[0, 0, 4:5, :]) + mod_ref[0, 0, 3:4, :]
    xb_ref[0] = xb.astype(xb_ref.dtype)
    for j, words in enumerate(_pack_rows(xb)):
        xrow_ref[0, pl.ds(j, xb.shape[0], stride=TOK_ROWS), :] = words

    logits = jnp.dot(xb, rw_ref[...], preferred_element_type=F32, precision=lax.Precision.HIGHEST)
    n_exp = logits.shape[1]
    per_group = n_exp // N_GROUPS
    groups_per_tile = LANE // per_group
    neg = -jnp.inf
    rt = ROUTE_ROWS
    lane_h = lax.broadcasted_iota(jnp.int32, (rt, LANE), 1)
    lane_hf = lane_h.astype(F32)
    lane_e = lax.broadcasted_iota(jnp.int32, (rt, n_exp), 1)
    lane_ef = lane_e.astype(F32)
    group_of = (lane_e // per_group).astype(F32)
    earlier = (lax.broadcasted_iota(jnp.int32, (rt, rt), 1)
               < lax.broadcasted_iota(jnp.int32, (rt, rt), 0)).astype(MXU_DTYPE)
    for part in range(logits.shape[0] // rt):
        rows = slice(part * rt, (part + 1) * rt)
        scores = _sigmoid(logits[rows])
        sel = scores + rb_ref[...]
        gscore = jnp.full((rt, LANE), neg, F32)
        for g in range(N_GROUPS):
            tile = sel[:, (g // groups_per_tile) * LANE:(g // groups_per_tile + 1) * LANE]
            mg = jnp.where(lane_h // per_group == g % groups_per_tile, tile, neg)
            m1 = jnp.max(mg, axis=1, keepdims=True)
            i1 = jnp.min(jnp.where(mg == m1, lane_hf, float(LANE)), axis=1, keepdims=True)
            m2 = jnp.max(jnp.where(lane_hf == i1, neg, mg), axis=1, keepdims=True)
            gscore = jnp.where(lane_h == g, m1 + m2, gscore)
        allowed = jnp.zeros((rt, n_exp), jnp.bool_)
        for _ in range(TOPK_GROUPS):
            m = jnp.max(gscore, axis=1, keepdims=True)
            gi = jnp.min(jnp.where(gscore == m, lane_hf, float(LANE)), axis=1, keepdims=True)
            gscore = jnp.where(lane_hf == gi, neg, gscore)
            allowed = allowed | (group_of == gi)
        cand = jnp.where(allowed, sel, neg)
        idx_out = jnp.zeros((rt, LANE), F32)
        w_out = jnp.zeros((rt, LANE), F32)
        chosen = jnp.zeros((rt, n_exp), F32)
        picks = []
        for k in range(TOP_K):
            m = jnp.max(cand, axis=1, keepdims=True)
            ik = jnp.min(jnp.where(cand == m, lane_ef, float(n_exp)), axis=1, keepdims=True)
            hit = lane_ef == ik
            wk = jnp.sum(jnp.where(hit, scores, 0.0), axis=1, keepdims=True)
            cand = jnp.where(hit, neg, cand)
            chosen = jnp.where(hit, 1.0, chosen)
            picks.append(ik)
            idx_out = jnp.where(lane_h == k, ik, idx_out)
            w_out = jnp.where(lane_h == k, wk, w_out)
        w_sum = jnp.sum(w_out, axis=1, keepdims=True)
        idx_ref[0, rows, :] = idx_out.astype(jnp.int32)
        wt_ref[0, rows, :] = w_out / w_sum * ROUTED_SCALE

        before = jnp.dot(earlier, chosen.astype(MXU_DTYPE), preferred_element_type=F32) + counts[...]
        rank_out = jnp.zeros((rt, LANE), F32)
        for k in range(TOP_K):
            rk = jnp.sum(jnp.where(lane_ef == picks[k], before, 0.0), axis=1, keepdims=True)
            rank_out = jnp.where(lane_h == k, rk, rank_out)
        rank_ref[0, rows, :] = rank_out.astype(jnp.int32)
        counts[...] = counts[...] + jnp.sum(chosen, axis=0, keepdims=True)
    cnt_ref[...] = counts[...].astype(jnp.int32)


def _outproj_router(y_rg, y_m, y_cm, xall, mod, g2, w_r, w_m, w_c, router_w, router_b, ctx_tiles, tile0):
    b, s_in, d = xall.shape
    tm = TOKEN_TILE
    s = s_in - tile0 * tm
    n_exp = router_w.shape[1]
    tok = lambda bi, i: (bi, i, 0)
    src = lambda bi, i: (bi, i + tile0, 0)
    full2 = lambda bi, i: (0, 0)
    return pl.pallas_call(
        _outproj_router_kernel,
        grid=(b, s // tm),
        in_specs=[pl.BlockSpec((1, tm, y_rg.shape[2]), src),
                  pl.BlockSpec((1, tm, y_m.shape[2]), src),
                  pl.BlockSpec((1, tm, y_cm.shape[2]), src),
                  pl.BlockSpec((1, tm, d), src),
                  pl.BlockSpec((1, 1, 8, d),
                               lambda bi, i: (bi, jnp.minimum((i + tile0) // ctx_tiles, 1), 0, 0)),
                  pl.BlockSpec((1, d), full2),
                  pl.BlockSpec(w_r.shape, full2),
                  pl.BlockSpec(w_m.shape, full2),
                  pl.BlockSpec(w_c.shape, full2),
                  pl.BlockSpec(router_w.shape, full2),
                  pl.BlockSpec((1, router_w.shape[1]), full2)],
        out_specs=[pl.BlockSpec((1, tm, d), tok),
                   pl.BlockSpec((1, tm, d), tok),
                   pl.BlockSpec((1, tm * TOK_ROWS, LANE), tok),
                   pl.BlockSpec((1, tm, LANE), tok),
                   pl.BlockSpec((1, tm, LANE), tok),
                   pl.BlockSpec((1, tm, LANE), tok),
                   pl.BlockSpec((1, n_exp), full2)],
        out_shape=[jax.ShapeDtypeStruct((b, s, d), F32),
                   jax.ShapeDtypeStruct((b, s, d), MXU_DTYPE),
                   jax.ShapeDtypeStruct((b, s * TOK_ROWS, LANE), ROW_DTYPE),
                   jax.ShapeDtypeStruct((b, s, LANE), jnp.int32),
                   jax.ShapeDtypeStruct((b, s, LANE), F32),
                   jax.ShapeDtypeStruct((b, s, LANE), jnp.int32),
                   jax.ShapeDtypeStruct((1, n_exp), jnp.int32)],
        scratch_shapes=[pltpu.VMEM((1, n_exp), F32)],
        compiler_params=_cparams("arbitrary", "arbitrary"),
        name="outproj_router",
    )(y_rg, y_m, y_cm, xall, mod, g2, w_r, w_m, w_c, router_w, router_b)


def _dest_kernel(idx_ref, rank_ref, ps_ref, dest_ref):
    idx = idx_ref[0].astype(F32)
    tm = idx.shape[0]
    n_exp = ps_ref.shape[1]
    lane_h = lax.broadcasted_iota(jnp.int32, (tm, LANE), 1)
    lane_ef = lax.broadcasted_iota(jnp.int32, (tm, n_exp), 1).astype(F32)
    start = jnp.zeros((tm, LANE), F32)
    for k in range(TOP_K):
        sk = jnp.sum(jnp.where(lane_ef == idx[:, k:k + 1], ps_ref[...], 0.0), axis=1, keepdims=True)
        start = jnp.where(lane_h == k, sk, start)
    dest_ref[0] = start.astype(jnp.int32) + rank_ref[0]


def _dest_rows(eidx, rank, pad_start):
    b, s, _ = eidx.shape
    tm = TOKEN_TILE
    tok = lambda bi, i: (bi, i, 0)
    return pl.pallas_call(
        _dest_kernel,
        grid=(b, s // tm),
        in_specs=[pl.BlockSpec((1, tm, LANE), tok), pl.BlockSpec((1, tm, LANE), tok),
                  pl.BlockSpec(pad_start.shape, lambda bi, i: (0, 0))],
        out_specs=pl.BlockSpec((1, tm, LANE), tok),
        out_shape=jax.ShapeDtypeStruct((b, s, LANE), jnp.int32),
        compiler_params=_cparams("parallel", "arbitrary"),
        name="dest_rows",
    )(eidx, rank, pad_start)


def _row_slab(ref, row):
    return ref.at[pl.ds(pl.multiple_of(row * TOK_ROWS, TOK_ROWS), TOK_ROWS), :]


def _dispatch_kernel(zb_ref, nz_ref, dest_ref, xrow_ref, xs_hbm, zbuf, sem, zsem):
    tm = xrow_ref.shape[1] // TOK_ROWS
    zrows = zbuf.shape[0]

    @pl.when((pl.program_id(0) == 0) & (pl.program_id(1) == 0))
    def _():
        zbuf[...] = jnp.zeros_like(zbuf)

        def z_start(n, carry):
            dst = xs_hbm.at[pl.ds(pl.multiple_of(zb_ref[n] * zrows, zrows), zrows), :]
            pltpu.make_async_copy(zbuf, dst, zsem).start()
            return carry

        def z_wait(n, carry):
            pltpu.make_async_copy(zbuf, xs_hbm.at[pl.ds(0, zrows), :], zsem).wait()
            return carry

        lax.fori_loop(0, nz_ref[0], z_start, 0)
        lax.fori_loop(0, nz_ref[0], z_wait, 0)

    def issue(r, carry):
        src = xrow_ref.at[0, pl.ds(pl.multiple_of(r * TOK_ROWS, TOK_ROWS), TOK_ROWS), :]
        for k in range(TOP_K):
            pltpu.make_async_copy(src, _row_slab(xs_hbm, dest_ref[0, 0, r * TOP_K + k]),
                                  sem).start(priority=k % 2)
        return carry

    lax.fori_loop(0, tm, issue, 0)
    n_all = tm * TOP_K * TOK_ROWS
    pltpu.make_async_copy(xs_hbm.at[pl.ds(0, n_all), :], xs_hbm.at[pl.ds(0, n_all), :], sem).wait()


def _dispatch(pad_blocks, n_pad_blocks, dest_tiles, xrow, n_rows):
    b, s_rows, _ = xrow.shape
    tm = TOKEN_TILE
    tiles = s_rows // TOK_ROWS // tm
    grid_spec = pltpu.PrefetchScalarGridSpec(
        num_scalar_prefetch=2,
        grid=(b, tiles),
        in_specs=[pl.BlockSpec((1, 1, tm * TOP_K), lambda bi, i, zb, nz: (bi * tiles + i, 0, 0),
                               memory_space=pltpu.SMEM),
                  pl.BlockSpec((1, tm * TOK_ROWS, LANE), lambda bi, i, zb, nz: (bi, i, 0))],
        out_specs=pl.BlockSpec(memory_space=pl.ANY),
        scratch_shapes=[pltpu.VMEM((MOE_BLOCK * TOK_ROWS, LANE), ROW_DTYPE),
                        pltpu.SemaphoreType.DMA(()), pltpu.SemaphoreType.DMA(())],
    )
    return pl.pallas_call(
        _dispatch_kernel,
        grid_spec=grid_spec,
        out_shape=jax.ShapeDtypeStruct((n_rows * TOK_ROWS, LANE), ROW_DTYPE),
        compiler_params=_cparams("arbitrary", "arbitrary"),
        name="dispatch",
    )(pad_blocks, n_pad_blocks, dest_tiles, xrow)


def _moe_ffn_kernel(be_ref, nu_ref, xs_ref, wg_ref, wu_ref, wd_ref, out_ref, wgu, wdb):
    i = pl.program_id(0)
    bm = xs_ref.shape[0] // TOK_ROWS
    ff = wg_ref.shape[2]
    ffp = wdb.shape[0]
    first = (i == 0) | (be_ref[i] != be_ref[jnp.maximum(i - 1, 0)])

    @pl.when(i == 0)
    def _():
        wgu[...] = jnp.zeros_like(wgu)
        wdb[...] = jnp.zeros_like(wdb)

    @pl.when(first)
    def _():
        wgu[0:ff, :] = wg_ref[0, 0].astype(MXU_DTYPE)
        wgu[ffp:ffp + ff, :] = wu_ref[0, 0].astype(MXU_DTYPE)
        wdb[0:ff, :] = wd_ref[0, 0].astype(MXU_DTYPE)

    @pl.when(i < nu_ref[0])
    def _():
        x = _unpack_rows([xs_ref[pl.ds(j, bm, stride=TOK_ROWS), :] for j in range(TOK_ROWS)])
        r = lax.dot_general(x.astype(MXU_DTYPE), wgu[...], (((1,), (1,)), ((), ())),
                            preferred_element_type=F32)
        hg = r[:, 0:ffp]
        h = (hg * _sigmoid(hg) * r[:, ffp:2 * ffp]).astype(MXU_DTYPE)
        y = jnp.dot(h, wdb[...], preferred_element_type=F32)
        for j, words in enumerate(_pack_rows(y)):
            out_ref[pl.ds(j, bm, stride=TOK_ROWS), :] = words

    @pl.when(i >= nu_ref[0])
    def _():
        out_ref[...] = jnp.zeros_like(out_ref)


def _moe_ffn(block_e, n_used, xs, e_gate_t, e_up_t, e_down, layer):
    bm = MOE_BLOCK
    n_blocks = xs.shape[0] // TOK_ROWS // bm
    _, _, ff, d = e_down.shape
    ffp = -(-ff // LANE) * LANE
    used = lambda i, be, nu: (jnp.minimum(i, nu[0] - 1), 0)
    expert = lambda i, be, nu: (layer, be[i], 0, 0)
    grid_spec = pltpu.PrefetchScalarGridSpec(
        num_scalar_prefetch=2,
        grid=(n_blocks,),
        in_specs=[pl.BlockSpec((bm * TOK_ROWS, LANE), used),
                  pl.BlockSpec((1, 1, ff, d), expert),
                  pl.BlockSpec((1, 1, ff, d), expert),
                  pl.BlockSpec((1, 1, ff, d), expert)],
        out_specs=pl.BlockSpec((bm * TOK_ROWS, LANE), lambda i, be, nu: (i, 0)),
        scratch_shapes=[pltpu.VMEM((2 * ffp, d), MXU_DTYPE), pltpu.VMEM((ffp, d), MXU_DTYPE)],
    )
    return pl.pallas_call(
        _moe_ffn_kernel,
        grid_spec=grid_spec,
        out_shape=jax.ShapeDtypeStruct(xs.shape, ROW_DTYPE),
        compiler_params=_cparams("arbitrary"),
        name="moe_ffn",
    )(block_e, n_used, xs, e_gate_t, e_up_t, e_down)


def _ffn_out_kernel(dest_ref, x1_ref, xb_ref, wt_ref, mod_ref, sg_ref, su_ref, sd_ref, fg_ref, ys_hbm,
                    out_ref, buf, sem, *, final):
    tm = x1_ref.shape[1]

    def issue(r, carry):
        for k in range(TOP_K):
            pltpu.make_async_copy(_row_slab(ys_hbm, dest_ref[0, 0, r * TOP_K + k]),
                                  _row_slab(buf, k * tm + r), sem).start(priority=k % 2)
        return carry

    lax.fori_loop(0, tm, issue, 0)
    xb = xb_ref[0]
    hg = jnp.dot(xb, sg_ref[...], preferred_element_type=F32)
    hu = jnp.dot(xb, su_ref[...], preferred_element_type=F32)
    y_sh = _dot(hg * _sigmoid(hg) * hu, sd_ref[...])
    pltpu.make_async_copy(ys_hbm.at[pl.ds(0, buf.shape[0]), :], buf, sem).wait()
    w = wt_ref[0]
    rows = []
    for j in range(TOK_ROWS):
        acc = w[:, 0:1] * buf[pl.ds(j, tm, stride=TOK_ROWS), :]
        for k in range(1, TOP_K):
            acc = acc + w[:, k:k + 1] * buf[pl.ds(k * tm * TOK_ROWS + j, tm, stride=TOK_ROWS), :]
        rows.append(acc)
    x2 = x1_ref[0] + mod_ref[0, 0, 5:6, :] * (_unpack_rows(rows) + y_sh)
    if final:
        x2 = _rms(x2, fg_ref[...])
    out_ref[0] = x2


def _ffn_out(dest_tiles, x1, xb, wts, ys, mod, sg, su, sd, final_g, ctx_tiles, tile0, final):
    b, s, d = x1.shape
    tm = TOKEN_TILE
    tiles = s // tm
    tok = lambda bi, i: (bi, i, 0)
    full2 = lambda bi, i: (0, 0)
    return pl.pallas_call(
        functools.partial(_ffn_out_kernel, final=final),
        grid=(b, tiles),
        in_specs=[pl.BlockSpec((1, 1, tm * TOP_K), lambda bi, i: (bi * tiles + i, 0, 0),
                               memory_space=pltpu.SMEM),
                  pl.BlockSpec((1, tm, d), tok),
                  pl.BlockSpec((1, tm, d), tok),
                  pl.BlockSpec((1, tm, LANE), tok),
                  pl.BlockSpec((1, 1, 8, d),
                               lambda bi, i: (bi, jnp.minimum((i + tile0) // ctx_tiles, 1), 0, 0)),
                  pl.BlockSpec(sg.shape, full2),
                  pl.BlockSpec(su.shape, full2),
                  pl.BlockSpec(sd.shape, full2),
                  pl.BlockSpec((1, d), full2),
                  pl.BlockSpec(memory_space=pl.ANY)],
        out_specs=pl.BlockSpec((1, tm, d), tok),
        out_shape=jax.ShapeDtypeStruct((b, s, d), F32),
        scratch_shapes=[pltpu.VMEM((TOP_K * tm * TOK_ROWS, LANE), ROW_DTYPE), pltpu.SemaphoreType.DMA(())],
        compiler_params=_cparams("parallel", "arbitrary"),
        name="ffn_out",
    )(dest_tiles, x1, xb, wts, mod, sg, su, sd, final_g, ys)


def _pos_embed_2d(rows, dim):
    quarter = dim // 4
    omega = 1.0 / (10000.0 ** (jnp.arange(quarter, dtype=F32) / quarter))
    row = jnp.repeat(jnp.arange(rows, dtype=F32), GRID_W)
    col = jnp.tile(jnp.arange(GRID_W, dtype=F32), rows)

    def axis_embed(p):
        ang = p[:, None] * omega[None, :]
        return jnp.concatenate([jnp.sin(ang), jnp.cos(ang)], axis=-1)

    return jnp.concatenate([axis_embed(row), axis_embed(col)], axis=-1)


def _pad_heads(w, n_heads, axis):
    shape = w.shape
    hd = shape[axis] // n_heads
    w = w.reshape(shape[:axis] + (n_heads, hd) + shape[axis + 1:])
    pad = [(0, 0)] * w.ndim
    pad[axis + 1] = (0, LANE - hd)
    w = jnp.pad(w, pad)
    return w.reshape(shape[:axis] + (n_heads * LANE,) + shape[axis + 1:])


def _routing_tables(n_assign, counts, bm):
    n_exp = counts.shape[0]
    padded = (counts + bm - 1) // bm * bm
    pad_end = jnp.cumsum(padded)
    pad_start = pad_end - padded
    n_blocks = -(-(n_assign + n_exp * (bm - 1)) // bm)
    block_start = jnp.arange(n_blocks, dtype=jnp.int32) * bm
    block_e = jnp.minimum(jnp.sum(pad_end[None, :] <= block_start[:, None], axis=1), n_exp - 1)
    n_used = (pad_end[-1] // bm).reshape(1)
    has_pad = ((pad_end[block_e] == block_start + bm) & (counts[block_e] % bm != 0)) | (block_start >= pad_end[-1])
    slot = jnp.cumsum(has_pad) - 1
    block_id = jnp.arange(n_blocks, dtype=jnp.int32)
    pad_blocks = jnp.sum(jnp.where(has_pad[None, :] & (slot[None, :] == block_id[:, None]), block_id[None, :], 0),
                         axis=1)
    n_pad_blocks = jnp.sum(has_pad).reshape(1)
    return (pad_start.astype(F32).reshape(1, n_exp), block_e.astype(jnp.int32), n_used.astype(jnp.int32),
            pad_blocks.astype(jnp.int32), n_pad_blocks.astype(jnp.int32), n_blocks * bm)


def kernel(x, c, ctx, c_ctx, norm1_g, norm2_g, w_mod, b_mod, w_in, rg_conv_w, rg_conv_b, rg_gate_w, rg_gate_b, rg_lambda, mlstm_gate_b, mlstm_norm_g, cm_norm_g, cm_w, cm_b, w_out, router_w, router_b, exp_gate, exp_up, exp_down, sh_gate, sh_up, sh_down, final_norm_g):
    batch, seq, d = x.shape
    ctx_len = ctx.shape[1]
    depth = w_in.shape[0]
    rg_w = rg_conv_w.shape[2]
    m_w = mlstm_norm_g.shape[1]
    cm_width = cm_norm_g.shape[1]
    head_dim = m_w // M_HEADS
    rg_hd = rg_w // RG_HEADS
    n_exp = router_w.shape[2]
    n_gates = 4 * M_HEADS
    assert ctx_len % TOKEN_TILE == 0 and seq % TOKEN_TILE == 0
    assert rg_w % LANE == 0 and LANE % rg_hd == 0 and head_dim < LANE and cm_width % LANE == 0
    assert d == TOK_ROWS * LANE
    ctx_tiles = ctx_len // TOKEN_TILE
    s_all = ctx_len + seq

    xall = jnp.concatenate([ctx, x + _pos_embed_2d(seq // GRID_W, d).astype(x.dtype)[None]], axis=1)
    cond = jnp.concatenate([c, c_ctx[None], jnp.zeros((SUBLANE - (batch + 1) % SUBLANE, d), F32)], axis=0)

    exp_gate_t = jnp.swapaxes(exp_gate, 2, 3)
    exp_up_t = jnp.swapaxes(exp_up, 2, 3)

    out = None
    for l in range(depth):
        last = l == depth - 1
        sizes = (rg_w, rg_w, m_w, m_w, m_w, m_w, n_gates, 2 * cm_width)
        offs = np.cumsum((0,) + sizes)
        w_cols = [w_in[l][:, offs[i]:offs[i + 1]] for i in range(len(sizes))]
        w_gate = w_cols[6].reshape(d, 2, 2, M_HEADS)
        lane_pad = ((0, 0), (0, LANE - 2 * M_HEADS))
        w_p = jnp.concatenate(
            [w_cols[0], w_cols[1]] + [_pad_heads(w_cols[i], M_HEADS, 1) for i in (2, 3, 4, 5)]
            + [jnp.pad(w_gate[:, :, g, :].reshape(d, 2 * M_HEADS), lane_pad) for g in range(2)]
            + [w_cols[7]], axis=1).astype(MXU_DTYPE)
        splits = (2 * rg_w, 4 * M_HEADS * LANE, 2 * LANE, 2 * cm_width)
        hp = LANE // rg_hd
        gw = rg_gate_w[l].reshape(4, RG_HEADS // hp, hp, rg_hd, rg_hd)
        eye = jnp.eye(hp, dtype=F32)
        wg = jnp.einsum('gthij,hk->tghikj', gw, eye).reshape(RG_HEADS // hp, 4, LANE, LANE)
        wg = wg.transpose(0, 2, 1, 3).reshape(RG_HEADS // hp, LANE, 4 * LANE).astype(MXU_DTYPE)
        bg = rg_gate_b[l].reshape(4, RG_HEADS // hp, LANE).transpose(1, 0, 2).reshape(RG_HEADS // hp, 1, 4 * LANE)
        gate_b = jnp.pad(mlstm_gate_b[l].transpose(1, 0, 2).reshape(2, 2 * M_HEADS), lane_pad)
        m_norm_g = _pad_heads(mlstm_norm_g[l], M_HEADS, 0).reshape(M_HEADS, 1, LANE)
        cm_ws = cm_w[l].astype(MXU_DTYPE)
        cm_bmap = jnp.repeat(cm_b[l].T, cm_width // CM_GROUPS, axis=1)
        w_o = w_out[l]
        w_r = w_o[:rg_w].astype(MXU_DTYPE)
        w_m = _pad_heads(w_o[rg_w:rg_w + m_w], M_HEADS, 0).astype(MXU_DTYPE)
        w_c = w_o[rg_w + m_w:].astype(MXU_DTYPE)

        mod = _adaln(cond, w_mod, b_mod, l).reshape(cond.shape[0], 6, d)
        mod = jnp.pad(mod, ((0, 0), (0, 2), (0, 0)))
        mod = jnp.stack([jnp.broadcast_to(mod[batch], (batch, 8, d)), mod[:batch]], axis=1)

        rg, qkvo, gates, cm = _inproj(xall, mod, norm1_g[l].reshape(1, d), w_p, splits, ctx_tiles)
        y_rg = _rglru(rg, rg_conv_w[l], rg_conv_b[l].reshape(1, rg_w), wg, bg, rg_lambda[l], ctx_len)
        y_m = _mlstm(qkvo, gates, gate_b, m_norm_g, ctx_len, head_dim)
        y_cm = _gmlp(cm, cm_norm_g[l].reshape(1, cm_width), cm_ws, cm_bmap)
        tile0 = ctx_tiles if last else 0
        n_tok = batch * (s_all - tile0 * TOKEN_TILE)
        x1, xb, xrow, eidx, wts, rank, counts = _outproj_router(
            y_rg, y_m, y_cm, xall, mod, norm2_g[l].reshape(1, d), w_r, w_m, w_c, router_w[l],
            router_b[l].reshape(1, n_exp), ctx_tiles, tile0)

        pad_start, block_e, n_used, pad_blocks, n_pad_blocks, n_rows = _routing_tables(
            n_tok * TOP_K, counts[0], MOE_BLOCK)
        dest = _dest_rows(eidx, rank, pad_start)
        dest_tiles = dest.reshape(n_tok, LANE)[:, :TOP_K].reshape(n_tok // TOKEN_TILE, 1, TOKEN_TILE * TOP_K)
        xs = _dispatch(pad_blocks, n_pad_blocks, dest_tiles, xrow, n_rows)
        ys = _moe_ffn(block_e, n_used, xs, exp_gate_t, exp_up_t, exp_down, l)
        out = _ffn_out(dest_tiles, x1, xb, wts, ys, mod, sh_gate[l].astype(MXU_DTYPE),
                       sh_up[l].astype(MXU_DTYPE), sh_down[l].astype(MXU_DTYPE),
                       final_norm_g.reshape(1, d), ctx_tiles, tile0, last)
        xall = out
    return out
```

```python
import functools

import numpy as np
import jax
import jax.numpy as jnp
from jax import lax
from jax.experimental import pallas as pl
from jax.experimental.pallas import tpu as pltpu

F32 = jnp.float32
MXU_DTYPE = jnp.bfloat16
LANE = 128
SUBLANE = 8
VMEM_LIMIT = 56 * 1024 * 1024

GRID_W = 64
RG_HEADS = 6
RG_C = 8.0
M_HEADS = 4
M_CHUNK = 128
CM_GROUPS = 4
CM_CHUNK = 128
N_GROUPS = 8
TOPK_GROUPS = 4
TOP_K = 8
ROUTED_SCALE = 2.5
EPS = 1e-6

ROW_DTYPE = jnp.float32
TOK_ROWS = 8
TOKEN_TILE = 256
MOE_BLOCK = 256


def _cparams(*sem):
    return pltpu.CompilerParams(dimension_semantics=sem, vmem_limit_bytes=VMEM_LIMIT)


def _dot(a, b):
    return jnp.dot(a.astype(MXU_DTYPE), b.astype(MXU_DTYPE), preferred_element_type=F32)


def _sigmoid(x):
    return 1.0 / (1.0 + jnp.exp(-x))


def _gelu_tanh(x):
    return 0.5 * x * (1.0 + jnp.tanh(0.7978845608028654 * (x + 0.044715 * (x * x * x))))


def _softplus(x):
    return jnp.maximum(x, 0.0) + jnp.log(1.0 + jnp.exp(-jnp.abs(x)))


def _rms(x, g):
    return x * lax.rsqrt(jnp.mean(x * x, axis=-1, keepdims=True) + EPS) * g


def _pack_rows(x):
    return [x[:, j * LANE:(j + 1) * LANE] for j in range(TOK_ROWS)]


def _unpack_rows(rows):
    return jnp.concatenate(rows, axis=1)


def _adaln_kernel(c_ref, w_ref, b_ref, o_ref):
    c = c_ref[...]
    o_ref[...] = _dot(c * _sigmoid(c), w_ref[0]) + b_ref[0]


def _adaln(cond, w_mod, b_mod, layer):
    n, d = cond.shape
    depth, _, n_out = w_mod.shape
    tn = 1536
    return pl.pallas_call(
        _adaln_kernel,
        grid=(n_out // tn,),
        in_specs=[pl.BlockSpec((n, d), lambda j: (0, 0)),
                  pl.BlockSpec((1, d, tn), lambda j: (layer, 0, j)),
                  pl.BlockSpec((1, 1, tn), lambda j: (layer, 0, j))],
        out_specs=pl.BlockSpec((n, tn), lambda j: (0, j)),
        out_shape=jax.ShapeDtypeStruct((n, n_out), F32),
        compiler_params=_cparams("arbitrary"),
        name="adaln",
    )(cond, w_mod, b_mod.reshape(depth, 1, n_out))


def _inproj_kernel(x_ref, mod_ref, g_ref, w_ref, rg_ref, qkvo_ref, gt_ref, cm_ref, *, splits):
    x = x_ref[0]
    shift = mod_ref[0, 0, 0:1, :]
    scale = mod_ref[0, 0, 1:2, :]
    xa = (_rms(x, g_ref[...]) * (1.0 + scale) + shift).astype(MXU_DTYPE)
    a = 0
    for ref, width in zip((rg_ref, qkvo_ref, gt_ref, cm_ref), splits):
        ref[0] = jnp.dot(xa, w_ref[:, a:a + width], preferred_element_type=F32)
        a += width


def _inproj(xall, mod, g, w_p, splits, ctx_tiles):
    b, s, d = xall.shape
    tm = TOKEN_TILE
    n_p = w_p.shape[1]
    tok = lambda bi, i: (bi, i, 0)
    return pl.pallas_call(
        functools.partial(_inproj_kernel, splits=splits),
        grid=(b, s // tm),
        in_specs=[pl.BlockSpec((1, tm, d), tok),
                  pl.BlockSpec((1, 1, 8, d), lambda bi, i: (bi, jnp.minimum(i // ctx_tiles, 1), 0, 0)),
                  pl.BlockSpec((1, d), lambda bi, i: (0, 0)),
                  pl.BlockSpec((d, n_p), lambda bi, i: (0, 0))],
        out_specs=[pl.BlockSpec((1, tm, w), tok) for w in splits],
        out_shape=[jax.ShapeDtypeStruct((b, s, w), F32) for w in splits],
        compiler_params=_cparams("parallel", "arbitrary"),
        name="inproj",
    )(xall, mod, g, w_p)


def _rglru_kernel(rx_ref, rgx_ref, cw_ref, cb_ref, wg_ref, bg_ref, lam_ref, y_ref,
                  af, uf, ab, ub, *, ctx_len):
    s = rx_ref.shape[1]
    x = rx_ref[0]
    t = lax.broadcasted_iota(jnp.int32, (s, LANE), 0)
    is_lat = t >= ctx_len
    pos = jnp.where(is_lat, t - ctx_len, t)
    seg_len = jnp.where(is_lat, s - ctx_len, ctx_len)
    xm2 = jnp.where(pos >= 2, pltpu.roll(x, 2, 0), 0.0)
    xm1 = jnp.where(pos >= 1, pltpu.roll(x, 1, 0), 0.0)
    xp1 = jnp.where(pos < seg_len - 1, pltpu.roll(x, s - 1, 0), 0.0)
    cw = cw_ref[...]
    z = cw[0:1] * xm2 + cw[1:2] * xm1 + cw[2:3] * x + cw[3:4] * xp1 + cb_ref[...]
    pre = _dot(z, wg_ref[0]) + bg_ref[0]
    sp = _softplus(-lam_ref[...])
    for d, (a_ref, u_ref) in enumerate(((af, uf), (ab, ub))):
        r = _sigmoid(pre[:, (2 * d) * LANE:(2 * d + 1) * LANE])
        i = _sigmoid(pre[:, (2 * d + 1) * LANE:(2 * d + 2) * LANE])
        log_a = -RG_C * r * sp[d:d + 1]
        a_ref[...] = jnp.exp(log_a)
        u_ref[...] = jnp.sqrt(1.0 - jnp.exp(2.0 * log_a)) * (i * z)

    row = lax.broadcasted_iota(jnp.int32, (SUBLANE, LANE), 0)
    n_tiles = s // SUBLANE
    n_ctx = ctx_len // SUBLANE

    def body(n, carry):
        hf_prev, hb_prev = carry
        of = pl.multiple_of(n * SUBLANE, SUBLANE)
        a = af[pl.ds(of, SUBLANE), :]
        u = uf[pl.ds(of, SUBLANE), :]
        for sh in (1, 2, 4):
            a_sh = jnp.where(row >= sh, pltpu.roll(a, sh, 0), 1.0)
            u_sh = jnp.where(row >= sh, pltpu.roll(u, sh, 0), 0.0)
            u = a * u_sh + u
            a = a * a_sh
        h = a * hf_prev + u
        uf[pl.ds(of, SUBLANE), :] = h
        hf_new = h[SUBLANE - 1:SUBLANE, :]
        tb = jnp.where(n < n_ctx, n_ctx - 1 - n, n_tiles - 1 + n_ctx - n)
        ob = pl.multiple_of(tb * SUBLANE, SUBLANE)
        a = ab[pl.ds(ob, SUBLANE), :]
        u = ub[pl.ds(ob, SUBLANE), :]
        for sh in (1, 2, 4):
            a_sh = jnp.where(row < SUBLANE - sh, pltpu.roll(a, SUBLANE - sh, 0), 1.0)
            u_sh = jnp.where(row < SUBLANE - sh, pltpu.roll(u, SUBLANE - sh, 0), 0.0)
            u = a * u_sh + u
            a = a * a_sh
        h = a * hb_prev + u
        ub[pl.ds(ob, SUBLANE), :] = h
        return hf_new, h[0:1, :]

    zero = jnp.zeros((1, LANE), F32)
    lax.fori_loop(0, n_tiles, body, (zero, zero))
    y_ref[0] = (uf[...] + ub[...]) * _gelu_tanh(rgx_ref[0])


def _rglru(rg, conv_w, conv_b, wg, bg, lam, ctx_len):
    b, s, w2 = rg.shape
    nt = w2 // 2 // LANE
    return pl.pallas_call(
        functools.partial(_rglru_kernel, ctx_len=ctx_len),
        grid=(b, nt),
        in_specs=[pl.BlockSpec((1, s, LANE), lambda bi, j: (bi, 0, j)),
                  pl.BlockSpec((1, s, LANE), lambda bi, j: (bi, 0, nt + j)),
                  pl.BlockSpec((conv_w.shape[0], LANE), lambda bi, j: (0, j)),
                  pl.BlockSpec((1, LANE), lambda bi, j: (0, j)),
                  pl.BlockSpec((1, LANE, 4 * LANE), lambda bi, j: (j, 0, 0)),
                  pl.BlockSpec((1, 1, 4 * LANE), lambda bi, j: (j, 0, 0)),
                  pl.BlockSpec((2, LANE), lambda bi, j: (0, j))],
        out_specs=pl.BlockSpec((1, s, LANE), lambda bi, j: (bi, 0, j)),
        out_shape=jax.ShapeDtypeStruct((b, s, nt * LANE), F32),
        scratch_shapes=[pltpu.VMEM((s, LANE), F32)] * 4,
        compiler_params=_cparams("parallel", "arbitrary"),
        name="rglru",
    )(rg, rg, conv_w, conv_b, wg, bg, lam)


def _mlstm_kernel(q_ref, k_ref, v_ref, o_ref, gi_ref, gf_ref, gb_ref, ng_ref, out_ref,
                  packed, hf, hb, *, ctx_len, head_dim):
    s = q_ref.shape[1]
    ln = M_CHUNK
    n_chunks = s // ln
    n_ctx = ctx_len // ln
    n_local = q_ref.shape[2] // LANE
    head0 = pl.program_id(1) * n_local

    @pl.when(pl.program_id(1) == 0)
    def _():
        tmod = lax.broadcasted_iota(jnp.int32, (s, LANE), 0) % ln
        fwd = lax.broadcasted_iota(jnp.int32, (s, LANE), 1) < M_HEADS

        def scan(x, combine, fill):
            sh = 1
            while sh < ln:
                down = jnp.where(tmod >= sh, pltpu.roll(x, sh, 0), fill)
                up = jnp.where(tmod < ln - sh, pltpu.roll(x, s - sh, 0), fill)
                x = combine(x, jnp.where(fwd, down, up))
                sh *= 2
            return x

        bc = scan(-_softplus(-(gf_ref[0] + gb_ref[1:2, :])), jnp.add, 0.0)
        w = gi_ref[0] + gb_ref[0:1, :] - bc
        packed[0] = bc
        packed[1] = w
        packed[2] = scan(w, jnp.maximum, -jnp.inf)

    tt = lax.broadcasted_iota(jnp.int32, (ln, ln), 0)
    ss = lax.broadcasted_iota(jnp.int32, (ln, ln), 1)
    ones = jnp.ones((ln, LANE), MXU_DTYPE)
    k_scale = head_dim ** -0.5

    def chunk_step(c, j, state, rev):
        c_state, n_state, m_st = state
        o = pl.multiple_of(c * ln, ln)
        cols = slice(j * LANE, (j + 1) * LANE)
        pick = jnp.full((ln, LANE), rev * M_HEADS + j, jnp.int32) + head0
        bc, w, cm = [jnp.take_along_axis(packed[a, pl.ds(o, ln), :], pick, axis=1) for a in range(3)]
        w_row = jnp.transpose(w)
        mask = (ss >= tt) if rev else (ss <= tt)
        inter = bc + m_st
        m = jnp.maximum(bc + cm, inter)
        p = jnp.exp(jnp.where(mask, bc + w_row - m, -jnp.inf))
        q = q_ref[0, pl.ds(o, ln), cols].astype(MXU_DTYPE)
        kf = k_ref[0, pl.ds(o, ln), cols] * k_scale
        v = v_ref[0, pl.ds(o, ln), cols].astype(MXU_DTYPE)
        sc = lax.dot_general(q, kf.astype(MXU_DTYPE), (((1,), (1,)), ((), ())),
                             preferred_element_type=F32) * p
        sc_hi = sc.astype(MXU_DTYPE)
        sc_lo = (sc - sc_hi.astype(F32)).astype(MXU_DTYPE)
        e_inter = jnp.exp(inter - m)
        num = (jnp.dot(sc_hi, v, preferred_element_type=F32)
               + e_inter * jnp.dot(q, c_state.astype(MXU_DTYPE), preferred_element_type=F32))
        den = (jnp.dot(sc_hi, ones, preferred_element_type=F32)
               + jnp.dot(sc_lo, ones, preferred_element_type=F32)
               + e_inter * jnp.dot(q, n_state.astype(MXU_DTYPE), preferred_element_type=F32))
        h = num / jnp.maximum(jnp.abs(den), jnp.exp(-m))
        last = 0 if rev else ln - 1
        g = bc[last:last + 1, :]
        m_new = jnp.maximum(g + m_st, g + cm[last:last + 1, :])
        decay = jnp.exp(g + m_st - m_new)
        ks_t = jnp.transpose(kf * jnp.exp(g + w - m_new)).astype(MXU_DTYPE)
        c_new = decay * c_state + jnp.dot(ks_t, v, preferred_element_type=F32)
        n_new = decay * n_state + jnp.dot(ks_t, ones, preferred_element_type=F32)
        return h, (c_new, n_new, m_new)

    def body(n, carry):
        cr = jnp.where(n < n_ctx, n_ctx - 1 - n, n_chunks - 1 + n_ctx - n)
        new = []
        for j in range(n_local):
            cols = slice(j * LANE, (j + 1) * LANE)
            h, st_f = chunk_step(n, j, carry[2 * j], 0)
            hf[pl.ds(pl.multiple_of(n * ln, ln), ln), cols] = h
            h, st_b = chunk_step(cr, j, carry[2 * j + 1], 1)
            hb[pl.ds(pl.multiple_of(cr * ln, ln), ln), cols] = h
            new += [st_f, st_b]
        return tuple(new)

    zero = (jnp.zeros((LANE, LANE), F32), jnp.zeros((LANE, LANE), F32), jnp.zeros((1, LANE), F32))
    lax.fori_loop(0, n_chunks, body, (zero,) * (2 * n_local))

    for j in range(n_local):
        cols = slice(j * LANE, (j + 1) * LANE)
        hsum = hf[:, cols] + hb[:, cols]
        ms = jnp.sum(hsum * hsum, axis=1, keepdims=True) * (1.0 / head_dim)
        out_ref[0, :, cols] = _sigmoid(o_ref[0, :, cols]) * (hsum * lax.rsqrt(ms + EPS) * ng_ref[0, :, cols])


def _mlstm(qkvo, gates, gate_b, norm_g, ctx_len, head_dim):
    b, s, _ = qkvo.shape
    hps = 2
    steps = M_HEADS // hps
    w = hps * LANE
    blk = lambda off: pl.BlockSpec((1, s, w), lambda bi, hi: (bi, 0, off * steps + hi))
    return pl.pallas_call(
        functools.partial(_mlstm_kernel, ctx_len=ctx_len, head_dim=head_dim),
        grid=(b, steps),
        in_specs=[blk(0), blk(1), blk(2), blk(3),
                  pl.BlockSpec((1, s, LANE), lambda bi, hi: (bi, 0, 0)),
                  pl.BlockSpec((1, s, LANE), lambda bi, hi: (bi, 0, 1)),
                  pl.BlockSpec((2, LANE), lambda bi, hi: (0, 0)),
                  pl.BlockSpec((1, 1, w), lambda bi, hi: (hi, 0, 0))],
        out_specs=pl.BlockSpec((1, s, w), lambda bi, hi: (bi, 0, hi)),
        out_shape=jax.ShapeDtypeStruct((b, s, M_HEADS * LANE), F32),
        scratch_shapes=[pltpu.VMEM((3, s, LANE), F32), pltpu.VMEM((s, w), F32), pltpu.VMEM((s, w), F32)],
        compiler_params=_cparams("arbitrary", "arbitrary"),
        name="mlstm",
    )(qkvo, qkvo, qkvo, qkvo, gates, gates, gate_b, norm_g.reshape(steps, 1, w))


def _gmlp_kernel(cm_ref, ng_ref, ws_ref, bs_ref, y_ref, *, width):
    tm = cm_ref.shape[1]
    gd = width // CM_GROUPS
    lane = lax.broadcasted_iota(jnp.int32, (CM_CHUNK, width), 1)
    for c in range(tm // CM_CHUNK):
        rows = slice(c * CM_CHUNK, (c + 1) * CM_CHUNK)
        u = _gelu_tanh(cm_ref[0, rows, 0:width])
        v = _rms(_gelu_tanh(cm_ref[0, rows, width:2 * width]), ng_ref[...]).astype(MXU_DTYPE)
        mixed = bs_ref[...]
        for g in range(CM_GROUPS):
            mg = jnp.dot(ws_ref[g], v, preferred_element_type=F32)
            mixed = mixed + jnp.where((lane >= g * gd) & (lane < (g + 1) * gd), mg, 0.0)
        y_ref[0, rows, :] = u * mixed


def _gmlp(cm, norm_g, w_s, b_map):
    b, s, w2 = cm.shape
    width = w2 // 2
    tm = TOKEN_TILE
    return pl.pallas_call(
        functools.partial(_gmlp_kernel, width=width),
        grid=(b, s // tm),
        in_specs=[pl.BlockSpec((1, tm, w2), lambda bi, i: (bi, i, 0)),
                  pl.BlockSpec((1, width), lambda bi, i: (0, 0)),
                  pl.BlockSpec(w_s.shape, lambda bi, i: (0, 0, 0)),
                  pl.BlockSpec(b_map.shape, lambda bi, i: (0, 0))],
        out_specs=pl.BlockSpec((1, tm, width), lambda bi, i: (bi, i, 0)),
        out_shape=jax.ShapeDtypeStruct((b, s, width), F32),
        compiler_params=_cparams("parallel", "arbitrary"),
        name="gmlp",
    )(cm, norm_g, w_s, b_map)


def _outproj_router_kernel(yr_ref, ym_ref, yc_ref, x_ref, mod_ref, g_ref, wr_ref, wm_ref, wc_ref,
                           rw_ref, rb_ref, x1_ref, xb_ref, xrow_ref, idx_ref, wt_ref, rank_ref, cnt_ref,
                           counts):
    @pl.when((pl.program_id(0) == 0) & (pl.program_id(1) == 0))
    def _():
        counts[...] = jnp.zeros_like(counts)

    mix = (_dot(yr_ref[0], wr_ref[...]) + _dot(ym_ref[0], wm_ref[...]) + _dot(yc_ref[0], wc_ref[...]))
    x1 = x_ref[0] + mod_ref[0, 0, 2:3, :] * mix
    x1_ref[0] = x1
    xb = _rms(x1, g_ref[...]) * (1.0 + mod_ref[0, 0, 4:5, :]) + mod_ref[0, 0, 3:4, :]
    xb_ref[0] = xb.astype(xb_ref.dtype)
    for j, words in enumerate(_pack_rows(xb)):
        xrow_ref[0, pl.ds(j, xb.shape[0], stride=TOK_ROWS), :] = words

    logits = jnp.dot(xb, rw_ref[...], preferred_element_type=F32, precision=lax.Precision.HIGHEST)
    tm, n_exp = logits.shape
    per_group = n_exp // N_GROUPS
    neg = -jnp.inf
    scores = _sigmoid(jnp.transpose(logits))
    sel = scores + rb_ref[...]
    row_e = lax.broadcasted_iota(jnp.int32, (n_exp, tm), 0)
    row_ef = row_e.astype(F32)
    row_g = lax.broadcasted_iota(jnp.int32, (per_group, tm), 0).astype(F32)
    row_k = lax.broadcasted_iota(jnp.int32, (SUBLANE, tm), 0)
    row_kf = row_k.astype(F32)

    gscore = jnp.full((SUBLANE, tm), neg, F32)
    for g in range(N_GROUPS):
        blk = sel[g * per_group:(g + 1) * per_group, :]
        m1 = jnp.max(blk, axis=0, keepdims=True)
        i1 = jnp.min(jnp.where(blk == m1, row_g, float(per_group)), axis=0, keepdims=True)
        m2 = jnp.max(jnp.where(row_g == i1, neg, blk), axis=0, keepdims=True)
        gscore = jnp.where(row_k == g, m1 + m2, gscore)
    group_of = (row_e // per_group).astype(F32)
    allowed = jnp.zeros((n_exp, tm), F32)
    for _ in range(TOPK_GROUPS):
        m = jnp.max(gscore, axis=0, keepdims=True)
        gi = jnp.min(jnp.where(gscore == m, row_kf, float(SUBLANE)), axis=0, keepdims=True)
        gscore = jnp.where(row_kf == gi, neg, gscore)
        allowed = jnp.where(group_of == gi, 1.0, allowed)
    cand = jnp.where(allowed > 0.0, sel, neg)
    idx_out = jnp.zeros((SUBLANE, tm), F32)
    w_out = jnp.zeros((SUBLANE, tm), F32)
    chosen = jnp.zeros((n_exp, tm), F32)
    picks = []
    for k in range(TOP_K):
        m = jnp.max(cand, axis=0, keepdims=True)
        ik = jnp.min(jnp.where(cand == m, row_ef, float(n_exp)), axis=0, keepdims=True)
        hit = row_ef == ik
        wk = jnp.sum(jnp.where(hit, scores, 0.0), axis=0, keepdims=True)
        cand = jnp.where(hit, neg, cand)
        chosen = jnp.where(hit, 1.0, chosen)
        picks.append(ik)
        idx_out = jnp.where(row_k == k, ik, idx_out)
        w_out = jnp.where(row_k == k, wk, w_out)
    w_sum = jnp.sum(w_out, axis=0, keepdims=True)
    idx_ref[0] = idx_out.astype(jnp.int32)
    wt_ref[0] = w_out / w_sum * ROUTED_SCALE

    earlier = (lax.broadcasted_iota(jnp.int32, (tm, tm), 0)
               < lax.broadcasted_iota(jnp.int32, (tm, tm), 1)).astype(MXU_DTYPE)
    chosen_m = chosen.astype(MXU_DTYPE)
    before = (jnp.dot(chosen_m, earlier, preferred_element_type=F32)
              + jnp.concatenate([counts[...]] * (tm // LANE), axis=1))
    rank_out = jnp.zeros((SUBLANE, tm), F32)
    for k in range(TOP_K):
        rk = jnp.sum(jnp.where(row_ef == picks[k], before, 0.0), axis=0, keepdims=True)
        rank_out = jnp.where(row_k == k, rk, rank_out)
    rank_ref[0] = rank_out.astype(jnp.int32)
    counts[...] = counts[...] + jnp.dot(chosen_m, jnp.ones((tm, LANE), MXU_DTYPE),
                                        preferred_element_type=F32)
    cnt_ref[...] = counts[...].astype(jnp.int32)


def _outproj_router(y_rg, y_m, y_cm, xall, mod, g2, w_r, w_m, w_c, router_w, router_b, ctx_tiles, tile0):
    b, s_in, d = xall.shape
    tm = TOKEN_TILE
    s = s_in - tile0 * tm
    n_exp = router_w.shape[1]
    tok = lambda bi, i: (bi, i, 0)
    src = lambda bi, i: (bi, i + tile0, 0)
    full2 = lambda bi, i: (0, 0)
    return pl.pallas_call(
        _outproj_router_kernel,
        grid=(b, s // tm),
        in_specs=[pl.BlockSpec((1, tm, y_rg.shape[2]), src),
                  pl.BlockSpec((1, tm, y_m.shape[2]), src),
                  pl.BlockSpec((1, tm, y_cm.shape[2]), src),
                  pl.BlockSpec((1, tm, d), src),
                  pl.BlockSpec((1, 1, 8, d),
                               lambda bi, i: (bi, jnp.minimum((i + tile0) // ctx_tiles, 1), 0, 0)),
                  pl.BlockSpec((1, d), full2),
                  pl.BlockSpec(w_r.shape, full2),
                  pl.BlockSpec(w_m.shape, full2),
                  pl.BlockSpec(w_c.shape, full2),
                  pl.BlockSpec(router_w.shape, full2),
                  pl.BlockSpec((n_exp, tm), full2)],
        out_specs=[pl.BlockSpec((1, tm, d), tok),
                   pl.BlockSpec((1, tm, d), tok),
                   pl.BlockSpec((1, tm * TOK_ROWS, LANE), tok),
                   pl.BlockSpec((1, SUBLANE, tm), lambda bi, i: (bi, 0, i)),
                   pl.BlockSpec((1, SUBLANE, tm), lambda bi, i: (bi, 0, i)),
                   pl.BlockSpec((1, SUBLANE, tm), lambda bi, i: (bi, 0, i)),
                   pl.BlockSpec((n_exp, LANE), full2)],
        out_shape=[jax.ShapeDtypeStruct((b, s, d), F32),
                   jax.ShapeDtypeStruct((b, s, d), MXU_DTYPE),
                   jax.ShapeDtypeStruct((b, s * TOK_ROWS, LANE), ROW_DTYPE),
                   jax.ShapeDtypeStruct((b, SUBLANE, s), jnp.int32),
                   jax.ShapeDtypeStruct((b, SUBLANE, s), F32),
                   jax.ShapeDtypeStruct((b, SUBLANE, s), jnp.int32),
                   jax.ShapeDtypeStruct((n_exp, LANE), jnp.int32)],
        scratch_shapes=[pltpu.VMEM((n_exp, LANE), F32)],
        compiler_params=_cparams("arbitrary", "arbitrary"),
        name="outproj_router",
    )(y_rg, y_m, y_cm, xall, mod, g2, w_r, w_m, w_c, router_w, router_b)


def _dest_kernel(idx_ref, rank_ref, ps_ref, dest_ref):
    idx = idx_ref[0].astype(F32)
    n_exp, tm = ps_ref.shape
    row_ef = lax.broadcasted_iota(jnp.int32, (n_exp, tm), 0).astype(F32)
    row_k = lax.broadcasted_iota(jnp.int32, (SUBLANE, tm), 0)
    start = jnp.zeros((SUBLANE, tm), F32)
    for k in range(TOP_K):
        sk = jnp.sum(jnp.where(row_ef == idx[k:k + 1, :], ps_ref[...], 0.0), axis=0, keepdims=True)
        start = jnp.where(row_k == k, sk, start)
    dest_ref[0] = start.astype(jnp.int32) + rank_ref[0]


def _dest_rows(eidx, rank, pad_start):
    b, _, s = eidx.shape
    tm = TOKEN_TILE
    tok = lambda bi, i: (bi, 0, i)
    return pl.pallas_call(
        _dest_kernel,
        grid=(b, s // tm),
        in_specs=[pl.BlockSpec((1, SUBLANE, tm), tok), pl.BlockSpec((1, SUBLANE, tm), tok),
                  pl.BlockSpec(pad_start.shape, lambda bi, i: (0, 0))],
        out_specs=pl.BlockSpec((1, SUBLANE, tm), tok),
        out_shape=jax.ShapeDtypeStruct((b, SUBLANE, s), jnp.int32),
        compiler_params=_cparams("parallel", "arbitrary"),
        name="dest_rows",
    )(eidx, rank, pad_start)


def _row_slab(ref, row):
    return ref.at[pl.ds(pl.multiple_of(row * TOK_ROWS, TOK_ROWS), TOK_ROWS), :]


def _dispatch_kernel(zb_ref, nz_ref, dest_ref, xrow_ref, xs_hbm, zbuf, sem, zsem):
    tm = xrow_ref.shape[1] // TOK_ROWS
    zrows = zbuf.shape[0]

    @pl.when((pl.program_id(0) == 0) & (pl.program_id(1) == 0))
    def _():
        zbuf[...] = jnp.zeros_like(zbuf)

        def z_start(n, carry):
            dst = xs_hbm.at[pl.ds(pl.multiple_of(zb_ref[n] * zrows, zrows), zrows), :]
            pltpu.make_async_copy(zbuf, dst, zsem).start()
            return carry

        def z_wait(n, carry):
            pltpu.make_async_copy(zbuf, xs_hbm.at[pl.ds(0, zrows), :], zsem).wait()
            return carry

        lax.fori_loop(0, nz_ref[0], z_start, 0)
        lax.fori_loop(0, nz_ref[0], z_wait, 0)

    def issue(r, carry):
        src = xrow_ref.at[0, pl.ds(pl.multiple_of(r * TOK_ROWS, TOK_ROWS), TOK_ROWS), :]
        for k in range(TOP_K):
            pltpu.make_async_copy(src, _row_slab(xs_hbm, dest_ref[0, 0, r * TOP_K + k]),
                                  sem).start(priority=k % 2)
        return carry

    lax.fori_loop(0, tm, issue, 0)
    n_all = tm * TOP_K * TOK_ROWS
    pltpu.make_async_copy(xs_hbm.at[pl.ds(0, n_all), :], xs_hbm.at[pl.ds(0, n_all), :], sem).wait()


def _dispatch(pad_blocks, n_pad_blocks, dest_tiles, xrow, n_rows):
    b, s_rows, _ = xrow.shape
    tm = TOKEN_TILE
    tiles = s_rows // TOK_ROWS // tm
    grid_spec = pltpu.PrefetchScalarGridSpec(
        num_scalar_prefetch=2,
        grid=(b, tiles),
        in_specs=[pl.BlockSpec((1, 1, tm * TOP_K), lambda bi, i, zb, nz: (bi * tiles + i, 0, 0),
                               memory_space=pltpu.SMEM),
                  pl.BlockSpec((1, tm * TOK_ROWS, LANE), lambda bi, i, zb, nz: (bi, i, 0))],
        out_specs=pl.BlockSpec(memory_space=pl.ANY),
        scratch_shapes=[pltpu.VMEM((MOE_BLOCK * TOK_ROWS, LANE), ROW_DTYPE),
                        pltpu.SemaphoreType.DMA(()), pltpu.SemaphoreType.DMA(())],
    )
    return pl.pallas_call(
        _dispatch_kernel,
        grid_spec=grid_spec,
        out_shape=jax.ShapeDtypeStruct((n_rows * TOK_ROWS, LANE), ROW_DTYPE),
        compiler_params=_cparams("arbitrary", "arbitrary"),
        name="dispatch",
    )(pad_blocks, n_pad_blocks, dest_tiles, xrow)


def _moe_ffn_kernel(be_ref, nu_ref, xs_ref, wg_ref, wu_ref, wd_ref, out_ref, wgu, wdb):
    i = pl.program_id(0)
    bm = xs_ref.shape[0] // TOK_ROWS
    ff = wg_ref.shape[2]
    ffp = wdb.shape[0]
    first = (i == 0) | (be_ref[i] != be_ref[jnp.maximum(i - 1, 0)])

    @pl.when(i == 0)
    def _():
        wgu[...] = jnp.zeros_like(wgu)
        wdb[...] = jnp.zeros_like(wdb)

    @pl.when(first)
    def _():
        wgu[0:ff, :] = wg_ref[0, 0].astype(MXU_DTYPE)
        wgu[ffp:ffp + ff, :] = wu_ref[0, 0].astype(MXU_DTYPE)
        wdb[0:ff, :] = wd_ref[0, 0].astype(MXU_DTYPE)

    @pl.when(i < nu_ref[0])
    def _():
        x = _unpack_rows([xs_ref[pl.ds(j, bm, stride=TOK_ROWS), :] for j in range(TOK_ROWS)])
        r = lax.dot_general(x.astype(MXU_DTYPE), wgu[...], (((1,), (1,)), ((), ())),
                            preferred_element_type=F32)
        hg = r[:, 0:ffp]
        h = (hg * _sigmoid(hg) * r[:, ffp:2 * ffp]).astype(MXU_DTYPE)
        y = jnp.dot(h, wdb[...], preferred_element_type=F32)
        for j, words in enumerate(_pack_rows(y)):
            out_ref[pl.ds(j, bm, stride=TOK_ROWS), :] = words

    @pl.when(i >= nu_ref[0])
    def _():
        out_ref[...] = jnp.zeros_like(out_ref)


def _moe_ffn(block_e, n_used, xs, e_gate_t, e_up_t, e_down, layer):
    bm = MOE_BLOCK
    n_blocks = xs.shape[0] // TOK_ROWS // bm
    _, _, ff, d = e_down.shape
    ffp = -(-ff // LANE) * LANE
    used = lambda i, be, nu: (jnp.minimum(i, nu[0] - 1), 0)
    expert = lambda i, be, nu: (layer, be[i], 0, 0)
    grid_spec = pltpu.PrefetchScalarGridSpec(
        num_scalar_prefetch=2,
        grid=(n_blocks,),
        in_specs=[pl.BlockSpec((bm * TOK_ROWS, LANE), used),
                  pl.BlockSpec((1, 1, ff, d), expert),
                  pl.BlockSpec((1, 1, ff, d), expert),
                  pl.BlockSpec((1, 1, ff, d), expert)],
        out_specs=pl.BlockSpec((bm * TOK_ROWS, LANE), lambda i, be, nu: (i, 0)),
        scratch_shapes=[pltpu.VMEM((2 * ffp, d), MXU_DTYPE), pltpu.VMEM((ffp, d), MXU_DTYPE)],
    )
    return pl.pallas_call(
        _moe_ffn_kernel,
        grid_spec=grid_spec,
        out_shape=jax.ShapeDtypeStruct(xs.shape, ROW_DTYPE),
        compiler_params=_cparams("arbitrary"),
        name="moe_ffn",
    )(block_e, n_used, xs, e_gate_t, e_up_t, e_down)


def _ffn_out_kernel(dest_ref, x1_ref, xb_ref, wt_ref, mod_ref, sg_ref, su_ref, sd_ref, fg_ref, ys_hbm,
                    out_ref, buf, sem, *, final):
    tm = x1_ref.shape[1]

    def issue(r, carry):
        for k in range(TOP_K):
            pltpu.make_async_copy(_row_slab(ys_hbm, dest_ref[0, 0, r * TOP_K + k]),
                                  _row_slab(buf, k * tm + r), sem).start(priority=k % 2)
        return carry

    lax.fori_loop(0, tm, issue, 0)
    xb = xb_ref[0]
    hg = jnp.dot(xb, sg_ref[...], preferred_element_type=F32)
    hu = jnp.dot(xb, su_ref[...], preferred_element_type=F32)
    y_sh = _dot(hg * _sigmoid(hg) * hu, sd_ref[...])
    pltpu.make_async_copy(ys_hbm.at[pl.ds(0, buf.shape[0]), :], buf, sem).wait()
    w = wt_ref[0]
    rows = []
    for j in range(TOK_ROWS):
        acc = w[:, 0:1] * buf[pl.ds(j, tm, stride=TOK_ROWS), :]
        for k in range(1, TOP_K):
            acc = acc + w[:, k:k + 1] * buf[pl.ds(k * tm * TOK_ROWS + j, tm, stride=TOK_ROWS), :]
        rows.append(acc)
    x2 = x1_ref[0] + mod_ref[0, 0, 5:6, :] * (_unpack_rows(rows) + y_sh)
    if final:
        x2 = _rms(x2, fg_ref[...])
    out_ref[0] = x2


def _ffn_out(dest_tiles, x1, xb, wts, ys, mod, sg, su, sd, final_g, ctx_tiles, tile0, final):
    b, s, d = x1.shape
    tm = TOKEN_TILE
    tiles = s // tm
    tok = lambda bi, i: (bi, i, 0)
    full2 = lambda bi, i: (0, 0)
    return pl.pallas_call(
        functools.partial(_ffn_out_kernel, final=final),
        grid=(b, tiles),
        in_specs=[pl.BlockSpec((1, 1, tm * TOP_K), lambda bi, i: (bi * tiles + i, 0, 0),
                               memory_space=pltpu.SMEM),
                  pl.BlockSpec((1, tm, d), tok),
                  pl.BlockSpec((1, tm, d), tok),
                  pl.BlockSpec((1, tm, LANE), tok),
                  pl.BlockSpec((1, 1, 8, d),
                               lambda bi, i: (bi, jnp.minimum((i + tile0) // ctx_tiles, 1), 0, 0)),
                  pl.BlockSpec(sg.shape, full2),
                  pl.BlockSpec(su.shape, full2),
                  pl.BlockSpec(sd.shape, full2),
                  pl.BlockSpec((1, d), full2),
                  pl.BlockSpec(memory_space=pl.ANY)],
        out_specs=pl.BlockSpec((1, tm, d), tok),
        out_shape=jax.ShapeDtypeStruct((b, s, d), F32),
        scratch_shapes=[pltpu.VMEM((TOP_K * tm * TOK_ROWS, LANE), ROW_DTYPE), pltpu.SemaphoreType.DMA(())],
        compiler_params=_cparams("parallel", "arbitrary"),
        name="ffn_out",
    )(dest_tiles, x1, xb, wts, mod, sg, su, sd, final_g, ys)


def _pos_embed_2d(rows, dim):
    quarter = dim // 4
    omega = 1.0 / (10000.0 ** (jnp.arange(quarter, dtype=F32) / quarter))
    row = jnp.repeat(jnp.arange(rows, dtype=F32), GRID_W)
    col = jnp.tile(jnp.arange(GRID_W, dtype=F32), rows)

    def axis_embed(p):
        ang = p[:, None] * omega[None, :]
        return jnp.concatenate([jnp.sin(ang), jnp.cos(ang)], axis=-1)

    return jnp.concatenate([axis_embed(row), axis_embed(col)], axis=-1)


def _pad_heads(w, n_heads, axis):
    shape = w.shape
    hd = shape[axis] // n_heads
    w = w.reshape(shape[:axis] + (n_heads, hd) + shape[axis + 1:])
    pad = [(0, 0)] * w.ndim
    pad[axis + 1] = (0, LANE - hd)
    w = jnp.pad(w, pad)
    return w.reshape(shape[:axis] + (n_heads * LANE,) + shape[axis + 1:])


def _routing_tables(n_assign, counts, bm):
    n_exp = counts.shape[0]
    padded = (counts + bm - 1) // bm * bm
    pad_end = jnp.cumsum(padded)
    pad_start = pad_end - padded
    n_blocks = -(-(n_assign + n_exp * (bm - 1)) // bm)
    block_start = jnp.arange(n_blocks, dtype=jnp.int32) * bm
    block_e = jnp.minimum(jnp.sum(pad_end[None, :] <= block_start[:, None], axis=1), n_exp - 1)
    n_used = (pad_end[-1] // bm).reshape(1)
    has_pad = ((pad_end[block_e] == block_start + bm) & (counts[block_e] % bm != 0)) | (block_start >= pad_end[-1])
    slot = jnp.cumsum(has_pad) - 1
    block_id = jnp.arange(n_blocks, dtype=jnp.int32)
    pad_blocks = jnp.sum(jnp.where(has_pad[None, :] & (slot[None, :] == block_id[:, None]), block_id[None, :], 0),
                         axis=1)
    n_pad_blocks = jnp.sum(has_pad).reshape(1)
    return (jnp.broadcast_to(pad_start.astype(F32)[:, None], (n_exp, TOKEN_TILE)),
            block_e.astype(jnp.int32), n_used.astype(jnp.int32),
            pad_blocks.astype(jnp.int32), n_pad_blocks.astype(jnp.int32), n_blocks * bm)


def kernel(x, c, ctx, c_ctx, norm1_g, norm2_g, w_mod, b_mod, w_in, rg_conv_w, rg_conv_b, rg_gate_w, rg_gate_b, rg_lambda, mlstm_gate_b, mlstm_norm_g, cm_norm_g, cm_w, cm_b, w_out, router_w, router_b, exp_gate, exp_up, exp_down, sh_gate, sh_up, sh_down, final_norm_g):
    batch, seq, d = x.shape
    ctx_len = ctx.shape[1]
    depth = w_in.shape[0]
    rg_w = rg_conv_w.shape[2]
    m_w = mlstm_norm_g.shape[1]
    cm_width = cm_norm_g.shape[1]
    head_dim = m_w // M_HEADS
    rg_hd = rg_w // RG_HEADS
    n_exp = router_w.shape[2]
    n_gates = 4 * M_HEADS
    assert ctx_len % TOKEN_TILE == 0 and seq % TOKEN_TILE == 0
    assert rg_w % LANE == 0 and LANE % rg_hd == 0 and head_dim < LANE and cm_width % LANE == 0
    assert d == TOK_ROWS * LANE
    assert TOP_K <= SUBLANE and N_GROUPS <= SUBLANE and n_exp % (N_GROUPS * SUBLANE) == 0
    ctx_tiles = ctx_len // TOKEN_TILE
    s_all = ctx_len + seq

    xall = jnp.concatenate([ctx, x + _pos_embed_2d(seq // GRID_W, d).astype(x.dtype)[None]], axis=1)
    cond = jnp.concatenate([c, c_ctx[None], jnp.zeros((SUBLANE - (batch + 1) % SUBLANE, d), F32)], axis=0)

    exp_gate_t = jnp.swapaxes(exp_gate, 2, 3)
    exp_up_t = jnp.swapaxes(exp_up, 2, 3)

    out = None
    for l in range(depth):
        last = l == depth - 1
        sizes = (rg_w, rg_w, m_w, m_w, m_w, m_w, n_gates, 2 * cm_width)
        offs = np.cumsum((0,) + sizes)
        w_cols = [w_in[l][:, offs[i]:offs[i + 1]] for i in range(len(sizes))]
        w_gate = w_cols[6].reshape(d, 2, 2, M_HEADS)
        lane_pad = ((0, 0), (0, LANE - 2 * M_HEADS))
        w_p = jnp.concatenate(
            [w_cols[0], w_cols[1]] + [_pad_heads(w_cols[i], M_HEADS, 1) for i in (2, 3, 4, 5)]
            + [jnp.pad(w_gate[:, :, g, :].reshape(d, 2 * M_HEADS), lane_pad) for g in range(2)]
            + [w_cols[7]], axis=1).astype(MXU_DTYPE)
        splits = (2 * rg_w, 4 * M_HEADS * LANE, 2 * LANE, 2 * cm_width)
        hp = LANE // rg_hd
        gw = rg_gate_w[l].reshape(4, RG_HEADS // hp, hp, rg_hd, rg_hd)
        eye = jnp.eye(hp, dtype=F32)
        wg = jnp.einsum('gthij,hk->tghikj', gw, eye).reshape(RG_HEADS // hp, 4, LANE, LANE)
        wg = wg.transpose(0, 2, 1, 3).reshape(RG_HEADS // hp, LANE, 4 * LANE).astype(MXU_DTYPE)
        bg = rg_gate_b[l].reshape(4, RG_HEADS // hp, LANE).transpose(1, 0, 2).reshape(RG_HEADS // hp, 1, 4 * LANE)
        gate_b = jnp.pad(mlstm_gate_b[l].transpose(1, 0, 2).reshape(2, 2 * M_HEADS), lane_pad)
        m_norm_g = _pad_heads(mlstm_norm_g[l], M_HEADS, 0).reshape(M_HEADS, 1, LANE)
        cm_ws = cm_w[l].astype(MXU_DTYPE)
        cm_bmap = jnp.repeat(cm_b[l].T, cm_width // CM_GROUPS, axis=1)
        w_o = w_out[l]
        w_r = w_o[:rg_w].astype(MXU_DTYPE)
        w_m = _pad_heads(w_o[rg_w:rg_w + m_w], M_HEADS, 0).astype(MXU_DTYPE)
        w_c = w_o[rg_w + m_w:].astype(MXU_DTYPE)

        mod = _adaln(cond, w_mod, b_mod, l).reshape(cond.shape[0], 6, d)
        mod = jnp.pad(mod, ((0, 0), (0, 2), (0, 0)))
        mod = jnp.stack([jnp.broadcast_to(mod[batch], (batch, 8, d)), mod[:batch]], axis=1)

        rg, qkvo, gates, cm = _inproj(xall, mod, norm1_g[l].reshape(1, d), w_p, splits, ctx_tiles)
        y_rg = _rglru(rg, rg_conv_w[l], rg_conv_b[l].reshape(1, rg_w), wg, bg, rg_lambda[l], ctx_len)
        y_m = _mlstm(qkvo, gates, gate_b, m_norm_g, ctx_len, head_dim)
        y_cm = _gmlp(cm, cm_norm_g[l].reshape(1, cm_width), cm_ws, cm_bmap)
        tile0 = ctx_tiles if last else 0
        n_tok = batch * (s_all - tile0 * TOKEN_TILE)
        router_bias = jnp.broadcast_to(router_b[l][:, None], (n_exp, TOKEN_TILE))
        x1, xb, xrow, eidx, wts, rank, counts = _outproj_router(
            y_rg, y_m, y_cm, xall, mod, norm2_g[l].reshape(1, d), w_r, w_m, w_c, router_w[l],
            router_bias, ctx_tiles, tile0)

        pad_start, block_e, n_used, pad_blocks, n_pad_blocks, n_rows = _routing_tables(
            n_tok * TOP_K, counts[:, 0], MOE_BLOCK)
        dest = _dest_rows(eidx, rank, pad_start)
        dest_tiles = dest[:, :TOP_K].transpose(0, 2, 1).reshape(n_tok // TOKEN_TILE, 1, TOKEN_TILE * TOP_K)
        wts = jnp.pad(wts[:, :TOP_K].transpose(0, 2, 1), ((0, 0), (0, 0), (0, LANE - TOP_K)))
        xs = _dispatch(pad_blocks, n_pad_blocks, dest_tiles, xrow, n_rows)
        ys = _moe_ffn(block_e, n_used, xs, exp_gate_t, exp_up_t, exp_down, l)
        out = _ffn_out(dest_tiles, x1, xb, wts, ys, mod, sh_gate[l].astype(MXU_DTYPE),
                       sh_up[l].astype(MXU_DTYPE), sh_down[l].astype(MXU_DTYPE),
                       final_norm_g.reshape(1, d), ctx_tiles, tile0, last)
        xall = out
    return out
```

```python
import functools

import numpy as np
import jax
import jax.numpy as jnp
from jax import lax
from jax.experimental import pallas as pl
from jax.experimental.pallas import tpu as pltpu

F32 = jnp.float32
MXU_DTYPE = jnp.bfloat16
LANE = 128
SUBLANE = 8
VMEM_LIMIT = 56 * 1024 * 1024

GRID_W = 64
RG_HEADS = 6
RG_C = 8.0
M_HEADS = 4
M_CHUNK = 128
CM_GROUPS = 4
CM_CHUNK = 128
N_GROUPS = 8
TOPK_GROUPS = 4
TOP_K = 8
ROUTED_SCALE = 2.5
EPS = 1e-6

ROW_DTYPE = jnp.float32
TOK_ROWS = 8
TOKEN_TILE = 256
MOE_BLOCK = 512


def _cparams(*sem):
    return pltpu.CompilerParams(dimension_semantics=sem, vmem_limit_bytes=VMEM_LIMIT)


def _dot(a, b):
    return jnp.dot(a.astype(MXU_DTYPE), b.astype(MXU_DTYPE), preferred_element_type=F32)


def _sigmoid(x):
    return 1.0 / (1.0 + jnp.exp(-x))


def _gelu_tanh(x):
    return 0.5 * x * (1.0 + jnp.tanh(0.7978845608028654 * (x + 0.044715 * (x * x * x))))


def _softplus(x):
    return jnp.maximum(x, 0.0) + jnp.log(1.0 + jnp.exp(-jnp.abs(x)))


def _rms(x, g):
    return x * lax.rsqrt(jnp.mean(x * x, axis=-1, keepdims=True) + EPS) * g


def _pack_rows(x):
    return [x[:, j * LANE:(j + 1) * LANE] for j in range(TOK_ROWS)]


def _unpack_rows(rows):
    return jnp.concatenate(rows, axis=1)


def _adaln_kernel(c_ref, w_ref, b_ref, o_ref):
    c = c_ref[...]
    o_ref[...] = _dot(c * _sigmoid(c), w_ref[0]) + b_ref[0]


def _adaln(cond, w_mod, b_mod, layer):
    n, d = cond.shape
    depth, _, n_out = w_mod.shape
    tn = 1536
    return pl.pallas_call(
        _adaln_kernel,
        grid=(n_out // tn,),
        in_specs=[pl.BlockSpec((n, d), lambda j: (0, 0)),
                  pl.BlockSpec((1, d, tn), lambda j: (layer, 0, j)),
                  pl.BlockSpec((1, 1, tn), lambda j: (layer, 0, j))],
        out_specs=pl.BlockSpec((n, tn), lambda j: (0, j)),
        out_shape=jax.ShapeDtypeStruct((n, n_out), F32),
        compiler_params=_cparams("arbitrary"),
        name="adaln",
    )(cond, w_mod, b_mod.reshape(depth, 1, n_out))


def _inproj_kernel(x_ref, mod_ref, g_ref, w_ref, rg_ref, qkvo_ref, gt_ref, cm_ref, *, splits):
    x = x_ref[0]
    shift = mod_ref[0, 0, 0:1, :]
    scale = mod_ref[0, 0, 1:2, :]
    xa = (_rms(x, g_ref[...]) * (1.0 + scale) + shift).astype(MXU_DTYPE)
    a = 0
    for ref, width in zip((rg_ref, qkvo_ref, gt_ref, cm_ref), splits):
        ref[0] = jnp.dot(xa, w_ref[:, a:a + width], preferred_element_type=F32)
        a += width


def _inproj(xall, mod, g, w_p, splits, ctx_tiles):
    b, s, d = xall.shape
    tm = TOKEN_TILE
    n_p = w_p.shape[1]
    tok = lambda bi, i: (bi, i, 0)
    return pl.pallas_call(
        functools.partial(_inproj_kernel, splits=splits),
        grid=(b, s // tm),
        in_specs=[pl.BlockSpec((1, tm, d), tok),
                  pl.BlockSpec((1, 1, 8, d), lambda bi, i: (bi, jnp.minimum(i // ctx_tiles, 1), 0, 0)),
                  pl.BlockSpec((1, d), lambda bi, i: (0, 0)),
                  pl.BlockSpec((d, n_p), lambda bi, i: (0, 0))],
        out_specs=[pl.BlockSpec((1, tm, w), tok) for w in splits],
        out_shape=[jax.ShapeDtypeStruct((b, s, w), F32) for w in splits],
        compiler_params=_cparams("parallel", "arbitrary"),
        name="inproj",
    )(xall, mod, g, w_p)


def _rglru_kernel(rx_ref, rgx_ref, cw_ref, cb_ref, wg_ref, bg_ref, lam_ref, y_ref,
                  af, uf, ab, ub, *, ctx_len):
    s = rx_ref.shape[1]
    x = rx_ref[0]
    t = lax.broadcasted_iota(jnp.int32, (s, LANE), 0)
    is_lat = t >= ctx_len
    pos = jnp.where(is_lat, t - ctx_len, t)
    seg_len = jnp.where(is_lat, s - ctx_len, ctx_len)
    xm2 = jnp.where(pos >= 2, pltpu.roll(x, 2, 0), 0.0)
    xm1 = jnp.where(pos >= 1, pltpu.roll(x, 1, 0), 0.0)
    xp1 = jnp.where(pos < seg_len - 1, pltpu.roll(x, s - 1, 0), 0.0)
    cw = cw_ref[...]
    z = cw[0:1] * xm2 + cw[1:2] * xm1 + cw[2:3] * x + cw[3:4] * xp1 + cb_ref[...]
    pre = _dot(z, wg_ref[0]) + bg_ref[0]
    sp = _softplus(-lam_ref[...])
    for d, (a_ref, u_ref) in enumerate(((af, uf), (ab, ub))):
        r = _sigmoid(pre[:, (2 * d) * LANE:(2 * d + 1) * LANE])
        i = _sigmoid(pre[:, (2 * d + 1) * LANE:(2 * d + 2) * LANE])
        log_a = -RG_C * r * sp[d:d + 1]
        a_ref[...] = jnp.exp(log_a)
        u_ref[...] = jnp.sqrt(1.0 - jnp.exp(2.0 * log_a)) * (i * z)

    row = lax.broadcasted_iota(jnp.int32, (SUBLANE, LANE), 0)
    n_tiles = s // SUBLANE
    n_ctx = ctx_len // SUBLANE

    def body(n, carry):
        hf_prev, hb_prev = carry
        of = pl.multiple_of(n * SUBLANE, SUBLANE)
        a = af[pl.ds(of, SUBLANE), :]
        u = uf[pl.ds(of, SUBLANE), :]
        for sh in (1, 2, 4):
            a_sh = jnp.where(row >= sh, pltpu.roll(a, sh, 0), 1.0)
            u_sh = jnp.where(row >= sh, pltpu.roll(u, sh, 0), 0.0)
            u = a * u_sh + u
            a = a * a_sh
        h = a * hf_prev + u
        uf[pl.ds(of, SUBLANE), :] = h
        hf_new = h[SUBLANE - 1:SUBLANE, :]
        tb = jnp.where(n < n_ctx, n_ctx - 1 - n, n_tiles - 1 + n_ctx - n)
        ob = pl.multiple_of(tb * SUBLANE, SUBLANE)
        a = ab[pl.ds(ob, SUBLANE), :]
        u = ub[pl.ds(ob, SUBLANE), :]
        for sh in (1, 2, 4):
            a_sh = jnp.where(row < SUBLANE - sh, pltpu.roll(a, SUBLANE - sh, 0), 1.0)
            u_sh = jnp.where(row < SUBLANE - sh, pltpu.roll(u, SUBLANE - sh, 0), 0.0)
            u = a * u_sh + u
            a = a * a_sh
        h = a * hb_prev + u
        ub[pl.ds(ob, SUBLANE), :] = h
        return hf_new, h[0:1, :]

    zero = jnp.zeros((1, LANE), F32)
    lax.fori_loop(0, n_tiles, body, (zero, zero))
    y_ref[0] = (uf[...] + ub[...]) * _gelu_tanh(rgx_ref[0])


def _rglru(rg, conv_w, conv_b, wg, bg, lam, ctx_len):
    b, s, w2 = rg.shape
    nt = w2 // 2 // LANE
    return pl.pallas_call(
        functools.partial(_rglru_kernel, ctx_len=ctx_len),
        grid=(b, nt),
        in_specs=[pl.BlockSpec((1, s, LANE), lambda bi, j: (bi, 0, j)),
                  pl.BlockSpec((1, s, LANE), lambda bi, j: (bi, 0, nt + j)),
                  pl.BlockSpec((conv_w.shape[0], LANE), lambda bi, j: (0, j)),
                  pl.BlockSpec((1, LANE), lambda bi, j: (0, j)),
                  pl.BlockSpec((1, LANE, 4 * LANE), lambda bi, j: (j, 0, 0)),
                  pl.BlockSpec((1, 1, 4 * LANE), lambda bi, j: (j, 0, 0)),
                  pl.BlockSpec((2, LANE), lambda bi, j: (0, j))],
        out_specs=pl.BlockSpec((1, s, LANE), lambda bi, j: (bi, 0, j)),
        out_shape=jax.ShapeDtypeStruct((b, s, nt * LANE), F32),
        scratch_shapes=[pltpu.VMEM((s, LANE), F32)] * 4,
        compiler_params=_cparams("parallel", "arbitrary"),
        name="rglru",
    )(rg, rg, conv_w, conv_b, wg, bg, lam)


def _mlstm_kernel(q_ref, k_ref, v_ref, o_ref, gi_ref, gf_ref, gb_ref, ng_ref, out_ref,
                  packed, hf, hb, *, ctx_len, head_dim):
    s = q_ref.shape[1]
    ln = M_CHUNK
    n_chunks = s // ln
    n_ctx = ctx_len // ln
    n_local = q_ref.shape[2] // LANE
    head0 = pl.program_id(1) * n_local

    @pl.when(pl.program_id(1) == 0)
    def _():
        tmod = lax.broadcasted_iota(jnp.int32, (s, LANE), 0) % ln
        fwd = lax.broadcasted_iota(jnp.int32, (s, LANE), 1) < M_HEADS

        def scan(x, combine, fill):
            sh = 1
            while sh < ln:
                down = jnp.where(tmod >= sh, pltpu.roll(x, sh, 0), fill)
                up = jnp.where(tmod < ln - sh, pltpu.roll(x, s - sh, 0), fill)
                x = combine(x, jnp.where(fwd, down, up))
                sh *= 2
            return x

        bc = scan(-_softplus(-(gf_ref[0] + gb_ref[1:2, :])), jnp.add, 0.0)
        w = gi_ref[0] + gb_ref[0:1, :] - bc
        packed[0] = bc
        packed[1] = w
        packed[2] = scan(w, jnp.maximum, -jnp.inf)

    tt = lax.broadcasted_iota(jnp.int32, (ln, ln), 0)
    ss = lax.broadcasted_iota(jnp.int32, (ln, ln), 1)
    ones = jnp.ones((ln, LANE), MXU_DTYPE)
    k_scale = head_dim ** -0.5

    def chunk_step(c, j, state, rev):
        c_state, n_state, m_st = state
        o = pl.multiple_of(c * ln, ln)
        cols = slice(j * LANE, (j + 1) * LANE)
        pick = jnp.full((ln, LANE), rev * M_HEADS + j, jnp.int32) + head0
        bc, w, cm = [jnp.take_along_axis(packed[a, pl.ds(o, ln), :], pick, axis=1) for a in range(3)]
        w_row = jnp.transpose(w)
        mask = (ss >= tt) if rev else (ss <= tt)
        inter = bc + m_st
        m = jnp.maximum(bc + cm, inter)
        p = jnp.exp(jnp.where(mask, bc + w_row - m, -jnp.inf))
        q = q_ref[0, pl.ds(o, ln), cols].astype(MXU_DTYPE)
        kf = k_ref[0, pl.ds(o, ln), cols] * k_scale
        v = v_ref[0, pl.ds(o, ln), cols].astype(MXU_DTYPE)
        sc = lax.dot_general(q, kf.astype(MXU_DTYPE), (((1,), (1,)), ((), ())),
                             preferred_element_type=F32) * p
        sc_hi = sc.astype(MXU_DTYPE)
        sc_lo = (sc - sc_hi.astype(F32)).astype(MXU_DTYPE)
        e_inter = jnp.exp(inter - m)
        num = (jnp.dot(sc_hi, v, preferred_element_type=F32)
               + e_inter * jnp.dot(q, c_state.astype(MXU_DTYPE), preferred_element_type=F32))
        den = (jnp.dot(sc_hi, ones, preferred_element_type=F32)
               + jnp.dot(sc_lo, ones, preferred_element_type=F32)
               + e_inter * jnp.dot(q, n_state.astype(MXU_DTYPE), preferred_element_type=F32))
        h = num / jnp.maximum(jnp.abs(den), jnp.exp(-m))
        last = 0 if rev else ln - 1
        g = bc[last:last + 1, :]
        m_new = jnp.maximum(g + m_st, g + cm[last:last + 1, :])
        decay = jnp.exp(g + m_st - m_new)
        ks_t = jnp.transpose(kf * jnp.exp(g + w - m_new)).astype(MXU_DTYPE)
        c_new = decay * c_state + jnp.dot(ks_t, v, preferred_element_type=F32)
        n_new = decay * n_state + jnp.dot(ks_t, ones, preferred_element_type=F32)
        return h, (c_new, n_new, m_new)

    def body(n, carry):
        cr = jnp.where(n < n_ctx, n_ctx - 1 - n, n_chunks - 1 + n_ctx - n)
        new = []
        for j in range(n_local):
            cols = slice(j * LANE, (j + 1) * LANE)
            h, st_f = chunk_step(n, j, carry[2 * j], 0)
            hf[pl.ds(pl.multiple_of(n * ln, ln), ln), cols] = h
            h, st_b = chunk_step(cr, j, carry[2 * j + 1], 1)
            hb[pl.ds(pl.multiple_of(cr * ln, ln), ln), cols] = h
            new += [st_f, st_b]
        return tuple(new)

    zero = (jnp.zeros((LANE, LANE), F32), jnp.zeros((LANE, LANE), F32), jnp.zeros((1, LANE), F32))
    lax.fori_loop(0, n_chunks, body, (zero,) * (2 * n_local))

    for j in range(n_local):
        cols = slice(j * LANE, (j + 1) * LANE)
        hsum = hf[:, cols] + hb[:, cols]
        ms = jnp.sum(hsum * hsum, axis=1, keepdims=True) * (1.0 / head_dim)
        out_ref[0, :, cols] = _sigmoid(o_ref[0, :, cols]) * (hsum * lax.rsqrt(ms + EPS) * ng_ref[0, :, cols])


def _mlstm(qkvo, gates, gate_b, norm_g, ctx_len, head_dim):
    b, s, _ = qkvo.shape
    hps = 2
    steps = M_HEADS // hps
    w = hps * LANE
    blk = lambda off: pl.BlockSpec((1, s, w), lambda bi, hi: (bi, 0, off * steps + hi))
    return pl.pallas_call(
        functools.partial(_mlstm_kernel, ctx_len=ctx_len, head_dim=head_dim),
        grid=(b, steps),
        in_specs=[blk(0), blk(1), blk(2), blk(3),
                  pl.BlockSpec((1, s, LANE), lambda bi, hi: (bi, 0, 0)),
                  pl.BlockSpec((1, s, LANE), lambda bi, hi: (bi, 0, 1)),
                  pl.BlockSpec((2, LANE), lambda bi, hi: (0, 0)),
                  pl.BlockSpec((1, 1, w), lambda bi, hi: (hi, 0, 0))],
        out_specs=pl.BlockSpec((1, s, w), lambda bi, hi: (bi, 0, hi)),
        out_shape=jax.ShapeDtypeStruct((b, s, M_HEADS * LANE), F32),
        scratch_shapes=[pltpu.VMEM((3, s, LANE), F32), pltpu.VMEM((s, w), F32), pltpu.VMEM((s, w), F32)],
        compiler_params=_cparams("arbitrary", "arbitrary"),
        name="mlstm",
    )(qkvo, qkvo, qkvo, qkvo, gates, gates, gate_b, norm_g.reshape(steps, 1, w))


def _gmlp_kernel(cm_ref, ng_ref, ws_ref, bs_ref, y_ref, *, width):
    tm = cm_ref.shape[1]
    gd = width // CM_GROUPS
    lane = lax.broadcasted_iota(jnp.int32, (CM_CHUNK, width), 1)
    for c in range(tm // CM_CHUNK):
        rows = slice(c * CM_CHUNK, (c + 1) * CM_CHUNK)
        u = _gelu_tanh(cm_ref[0, rows, 0:width])
        v = _rms(_gelu_tanh(cm_ref[0, rows, width:2 * width]), ng_ref[...]).astype(MXU_DTYPE)
        mixed = bs_ref[...]
        for g in range(CM_GROUPS):
            mg = jnp.dot(ws_ref[g], v, preferred_element_type=F32)
            mixed = mixed + jnp.where((lane >= g * gd) & (lane < (g + 1) * gd), mg, 0.0)
        y_ref[0, rows, :] = u * mixed


def _gmlp(cm, norm_g, w_s, b_map):
    b, s, w2 = cm.shape
    width = w2 // 2
    tm = TOKEN_TILE
    return pl.pallas_call(
        functools.partial(_gmlp_kernel, width=width),
        grid=(b, s // tm),
        in_specs=[pl.BlockSpec((1, tm, w2), lambda bi, i: (bi, i, 0)),
                  pl.BlockSpec((1, width), lambda bi, i: (0, 0)),
                  pl.BlockSpec(w_s.shape, lambda bi, i: (0, 0, 0)),
                  pl.BlockSpec(b_map.shape, lambda bi, i: (0, 0))],
        out_specs=pl.BlockSpec((1, tm, width), lambda bi, i: (bi, i, 0)),
        out_shape=jax.ShapeDtypeStruct((b, s, width), F32),
        compiler_params=_cparams("parallel", "arbitrary"),
        name="gmlp",
    )(cm, norm_g, w_s, b_map)


def _outproj_router_kernel(yr_ref, ym_ref, yc_ref, x_ref, mod_ref, g_ref, wr_ref, wm_ref, wc_ref,
                           rw_ref, rb_ref, x1_ref, xb_ref, xrow_ref, idx_ref, wt_ref, rank_ref, cnt_ref,
                           counts):
    @pl.when((pl.program_id(0) == 0) & (pl.program_id(1) == 0))
    def _():
        counts[...] = jnp.zeros_like(counts)

    mix = (_dot(yr_ref[0], wr_ref[...]) + _dot(ym_ref[0], wm_ref[...]) + _dot(yc_ref[0], wc_ref[...]))
    x1 = x_ref[0] + mod_ref[0, 0, 2:3, :] * mix
    x1_ref[0] = x1
    xb = _rms(x1, g_ref[...]) * (1.0 + mod_ref[0, 0, 4:5, :]) + mod_ref[0, 0, 3:4, :]
    xb_ref[0] = xb.astype(xb_ref.dtype)
    for j, words in enumerate(_pack_rows(xb)):
        xrow_ref[0, pl.ds(j, xb.shape[0], stride=TOK_ROWS), :] = words

    logits = jnp.dot(xb, rw_ref[...], preferred_element_type=F32, precision=lax.Precision.HIGHEST)
    tm, n_exp = logits.shape
    per_group = n_exp // N_GROUPS
    neg = -jnp.inf
    scores = _sigmoid(jnp.transpose(logits))
    sel = scores + rb_ref[...]
    row_e = lax.broadcasted_iota(jnp.int32, (n_exp, tm), 0)
    row_ef = row_e.astype(F32)
    row_g = lax.broadcasted_iota(jnp.int32, (per_group, tm), 0).astype(F32)
    row_k = lax.broadcasted_iota(jnp.int32, (SUBLANE, tm), 0)
    row_kf = row_k.astype(F32)

    gscore = jnp.full((SUBLANE, tm), neg, F32)
    for g in range(N_GROUPS):
        blk = sel[g * per_group:(g + 1) * per_group, :]
        m1 = jnp.max(blk, axis=0, keepdims=True)
        i1 = jnp.min(jnp.where(blk == m1, row_g, float(per_group)), axis=0, keepdims=True)
        m2 = jnp.max(jnp.where(row_g == i1, neg, blk), axis=0, keepdims=True)
        gscore = jnp.where(row_k == g, m1 + m2, gscore)
    group_of = (row_e // per_group).astype(F32)
    allowed = jnp.zeros((n_exp, tm), F32)
    for _ in range(TOPK_GROUPS):
        m = jnp.max(gscore, axis=0, keepdims=True)
        gi = jnp.min(jnp.where(gscore == m, row_kf, float(SUBLANE)), axis=0, keepdims=True)
        gscore = jnp.where(row_kf == gi, neg, gscore)
        allowed = jnp.where(group_of == gi, 1.0, allowed)
    cand = jnp.where(allowed > 0.0, sel, neg)
    idx_out = jnp.zeros((SUBLANE, tm), F32)
    w_out = jnp.zeros((SUBLANE, tm), F32)
    chosen = jnp.zeros((n_exp, tm), F32)
    picks = []
    for k in range(TOP_K):
        m = jnp.max(cand, axis=0, keepdims=True)
        ik = jnp.min(jnp.where(cand == m, row_ef, float(n_exp)), axis=0, keepdims=True)
        hit = row_ef == ik
        wk = jnp.sum(jnp.where(hit, scores, 0.0), axis=0, keepdims=True)
        cand = jnp.where(hit, neg, cand)
        chosen = jnp.where(hit, 1.0, chosen)
        picks.append(ik)
        idx_out = jnp.where(row_k == k, ik, idx_out)
        w_out = jnp.where(row_k == k, wk, w_out)
    w_sum = jnp.sum(w_out, axis=0, keepdims=True)
    idx_ref[0] = idx_out.astype(jnp.int32)
    wt_ref[0] = w_out / w_sum * ROUTED_SCALE

    earlier = (lax.broadcasted_iota(jnp.int32, (tm, tm), 0)
               < lax.broadcasted_iota(jnp.int32, (tm, tm), 1)).astype(MXU_DTYPE)
    chosen_m = chosen.astype(MXU_DTYPE)
    before = (jnp.dot(chosen_m, earlier, preferred_element_type=F32)
              + jnp.concatenate([counts[...]] * (tm // LANE), axis=1))
    rank_out = jnp.zeros((SUBLANE, tm), F32)
    for k in range(TOP_K):
        rk = jnp.sum(jnp.where(row_ef == picks[k], before, 0.0), axis=0, keepdims=True)
        rank_out = jnp.where(row_k == k, rk, rank_out)
    rank_ref[0] = rank_out.astype(jnp.int32)
    counts[...] = counts[...] + jnp.dot(chosen_m, jnp.ones((tm, LANE), MXU_DTYPE),
                                        preferred_element_type=F32)
    cnt_ref[...] = counts[...].astype(jnp.int32)


def _outproj_router(y_rg, y_m, y_cm, xall, mod, g2, w_r, w_m, w_c, router_w, router_b, ctx_tiles, tile0):
    b, s_in, d = xall.shape
    tm = TOKEN_TILE
    s = s_in - tile0 * tm
    n_exp = router_w.shape[1]
    tok = lambda bi, i: (bi, i, 0)
    src = lambda bi, i: (bi, i + tile0, 0)
    full2 = lambda bi, i: (0, 0)
    return pl.pallas_call(
        _outproj_router_kernel,
        grid=(b, s // tm),
        in_specs=[pl.BlockSpec((1, tm, y_rg.shape[2]), src),
                  pl.BlockSpec((1, tm, y_m.shape[2]), src),
                  pl.BlockSpec((1, tm, y_cm.shape[2]), src),
                  pl.BlockSpec((1, tm, d), src),
                  pl.BlockSpec((1, 1, 8, d),
                               lambda bi, i: (bi, jnp.minimum((i + tile0) // ctx_tiles, 1), 0, 0)),
                  pl.BlockSpec((1, d), full2),
                  pl.BlockSpec(w_r.shape, full2),
                  pl.BlockSpec(w_m.shape, full2),
                  pl.BlockSpec(w_c.shape, full2),
                  pl.BlockSpec(router_w.shape, full2),
                  pl.BlockSpec((n_exp, tm), full2)],
        out_specs=[pl.BlockSpec((1, tm, d), tok),
                   pl.BlockSpec((1, tm, d), tok),
                   pl.BlockSpec((1, tm * TOK_ROWS, LANE), tok),
                   pl.BlockSpec((1, SUBLANE, tm), lambda bi, i: (bi, 0, i)),
                   pl.BlockSpec((1, SUBLANE, tm), lambda bi, i: (bi, 0, i)),
                   pl.BlockSpec((1, SUBLANE, tm), lambda bi, i: (bi, 0, i)),
                   pl.BlockSpec((n_exp, LANE), full2)],
        out_shape=[jax.ShapeDtypeStruct((b, s, d), F32),
                   jax.ShapeDtypeStruct((b, s, d), MXU_DTYPE),
                   jax.ShapeDtypeStruct((b, s * TOK_ROWS, LANE), ROW_DTYPE),
                   jax.ShapeDtypeStruct((b, SUBLANE, s), jnp.int32),
                   jax.ShapeDtypeStruct((b, SUBLANE, s), F32),
                   jax.ShapeDtypeStruct((b, SUBLANE, s), jnp.int32),
                   jax.ShapeDtypeStruct((n_exp, LANE), jnp.int32)],
        scratch_shapes=[pltpu.VMEM((n_exp, LANE), F32)],
        compiler_params=_cparams("arbitrary", "arbitrary"),
        name="outproj_router",
    )(y_rg, y_m, y_cm, xall, mod, g2, w_r, w_m, w_c, router_w, router_b)


def _dest_kernel(idx_ref, rank_ref, ps_ref, dest_ref):
    idx = idx_ref[0].astype(F32)
    n_exp, tm = ps_ref.shape
    row_ef = lax.broadcasted_iota(jnp.int32, (n_exp, tm), 0).astype(F32)
    row_k = lax.broadcasted_iota(jnp.int32, (SUBLANE, tm), 0)
    start = jnp.zeros((SUBLANE, tm), F32)
    for k in range(TOP_K):
        sk = jnp.sum(jnp.where(row_ef == idx[k:k + 1, :], ps_ref[...], 0.0), axis=0, keepdims=True)
        start = jnp.where(row_k == k, sk, start)
    dest_ref[0] = start.astype(jnp.int32) + rank_ref[0]


def _dest_rows(eidx, rank, pad_start):
    b, _, s = eidx.shape
    tm = TOKEN_TILE
    tok = lambda bi, i: (bi, 0, i)
    return pl.pallas_call(
        _dest_kernel,
        grid=(b, s // tm),
        in_specs=[pl.BlockSpec((1, SUBLANE, tm), tok), pl.BlockSpec((1, SUBLANE, tm), tok),
                  pl.BlockSpec(pad_start.shape, lambda bi, i: (0, 0))],
        out_specs=pl.BlockSpec((1, SUBLANE, tm), tok),
        out_shape=jax.ShapeDtypeStruct((b, SUBLANE, s), jnp.int32),
        compiler_params=_cparams("parallel", "arbitrary"),
        name="dest_rows",
    )(eidx, rank, pad_start)


def _row_slab(ref, row):
    return ref.at[pl.ds(pl.multiple_of(row * TOK_ROWS, TOK_ROWS), TOK_ROWS), :]


def _dispatch_kernel(zb_ref, nz_ref, dest_ref, xrow_ref, xs_hbm, zbuf, sem, zsem):
    tm = xrow_ref.shape[1] // TOK_ROWS
    zrows = zbuf.shape[0]

    @pl.when((pl.program_id(0) == 0) & (pl.program_id(1) == 0))
    def _():
        zbuf[...] = jnp.zeros_like(zbuf)

        def z_start(n, carry):
            dst = xs_hbm.at[pl.ds(pl.multiple_of(zb_ref[n] * zrows, zrows), zrows), :]
            pltpu.make_async_copy(zbuf, dst, zsem).start()
            return carry

        def z_wait(n, carry):
            pltpu.make_async_copy(zbuf, xs_hbm.at[pl.ds(0, zrows), :], zsem).wait()
            return carry

        lax.fori_loop(0, nz_ref[0], z_start, 0)
        lax.fori_loop(0, nz_ref[0], z_wait, 0)

    def issue(r, carry):
        src = xrow_ref.at[0, pl.ds(pl.multiple_of(r * TOK_ROWS, TOK_ROWS), TOK_ROWS), :]
        for k in range(TOP_K):
            pltpu.make_async_copy(src, _row_slab(xs_hbm, dest_ref[0, 0, r * TOP_K + k]),
                                  sem).start(priority=k % 2)
        return carry

    lax.fori_loop(0, tm, issue, 0)
    n_all = tm * TOP_K * TOK_ROWS
    pltpu.make_async_copy(xs_hbm.at[pl.ds(0, n_all), :], xs_hbm.at[pl.ds(0, n_all), :], sem).wait()


def _dispatch(pad_blocks, n_pad_blocks, dest_tiles, xrow, n_rows):
    b, s_rows, _ = xrow.shape
    tm = TOKEN_TILE
    tiles = s_rows // TOK_ROWS // tm
    grid_spec = pltpu.PrefetchScalarGridSpec(
        num_scalar_prefetch=2,
        grid=(b, tiles),
        in_specs=[pl.BlockSpec((1, 1, tm * TOP_K), lambda bi, i, zb, nz: (bi * tiles + i, 0, 0),
                               memory_space=pltpu.SMEM),
                  pl.BlockSpec((1, tm * TOK_ROWS, LANE), lambda bi, i, zb, nz: (bi, i, 0))],
        out_specs=pl.BlockSpec(memory_space=pl.ANY),
        scratch_shapes=[pltpu.VMEM((MOE_BLOCK * TOK_ROWS, LANE), ROW_DTYPE),
                        pltpu.SemaphoreType.DMA(()), pltpu.SemaphoreType.DMA(())],
    )
    return pl.pallas_call(
        _dispatch_kernel,
        grid_spec=grid_spec,
        out_shape=jax.ShapeDtypeStruct((n_rows * TOK_ROWS, LANE), ROW_DTYPE),
        compiler_params=_cparams("arbitrary", "arbitrary"),
        name="dispatch",
    )(pad_blocks, n_pad_blocks, dest_tiles, xrow)


def _moe_ffn_kernel(be_ref, nu_ref, xs_ref, wg_ref, wu_ref, wd_ref, out_ref, wgu, wdb):
    i = pl.program_id(0)
    bm = xs_ref.shape[0] // TOK_ROWS
    ff = wg_ref.shape[2]
    ffp = wdb.shape[0]
    first = (i == 0) | (be_ref[i] != be_ref[jnp.maximum(i - 1, 0)])

    @pl.when(i == 0)
    def _():
        wgu[...] = jnp.zeros_like(wgu)
        wdb[...] = jnp.zeros_like(wdb)

    @pl.when(first)
    def _():
        wgu[0:ff, :] = wg_ref[0, 0].astype(MXU_DTYPE)
        wgu[ffp:ffp + ff, :] = wu_ref[0, 0].astype(MXU_DTYPE)
        wdb[0:ff, :] = wd_ref[0, 0].astype(MXU_DTYPE)

    @pl.when(i < nu_ref[0])
    def _():
        x = _unpack_rows([xs_ref[pl.ds(j, bm, stride=TOK_ROWS), :] for j in range(TOK_ROWS)])
        r = lax.dot_general(x.astype(MXU_DTYPE), wgu[...], (((1,), (1,)), ((), ())),
                            preferred_element_type=F32)
        hg = r[:, 0:ffp]
        h = (hg * _sigmoid(hg) * r[:, ffp:2 * ffp]).astype(MXU_DTYPE)
        y = jnp.dot(h, wdb[...], preferred_element_type=F32)
        for j, words in enumerate(_pack_rows(y)):
            out_ref[pl.ds(j, bm, stride=TOK_ROWS), :] = words

    @pl.when(i >= nu_ref[0])
    def _():
        out_ref[...] = jnp.zeros_like(out_ref)


def _moe_ffn(block_e, n_used, xs, e_gate_t, e_up_t, e_down, layer):
    bm = MOE_BLOCK
    n_blocks = xs.shape[0] // TOK_ROWS // bm
    _, _, ff, d = e_down.shape
    ffp = -(-ff // LANE) * LANE
    used = lambda i, be, nu: (jnp.minimum(i, nu[0] - 1), 0)
    expert = lambda i, be, nu: (layer, be[i], 0, 0)
    grid_spec = pltpu.PrefetchScalarGridSpec(
        num_scalar_prefetch=2,
        grid=(n_blocks,),
        in_specs=[pl.BlockSpec((bm * TOK_ROWS, LANE), used),
                  pl.BlockSpec((1, 1, ff, d), expert),
                  pl.BlockSpec((1, 1, ff, d), expert),
                  pl.BlockSpec((1, 1, ff, d), expert)],
        out_specs=pl.BlockSpec((bm * TOK_ROWS, LANE), lambda i, be, nu: (i, 0)),
        scratch_shapes=[pltpu.VMEM((2 * ffp, d), MXU_DTYPE), pltpu.VMEM((ffp, d), MXU_DTYPE)],
    )
    return pl.pallas_call(
        _moe_ffn_kernel,
        grid_spec=grid_spec,
        out_shape=jax.ShapeDtypeStruct(xs.shape, ROW_DTYPE),
        compiler_params=_cparams("arbitrary"),
        name="moe_ffn",
    )(block_e, n_used, xs, e_gate_t, e_up_t, e_down)


def _ffn_out_kernel(dest_ref, x1_ref, xb_ref, wt_ref, mod_ref, sg_ref, su_ref, sd_ref, fg_ref, ys_hbm,
                    out_ref, buf, sem, *, final):
    tm = x1_ref.shape[1]

    def issue(r, carry):
        for k in range(TOP_K):
            pltpu.make_async_copy(_row_slab(ys_hbm, dest_ref[0, 0, r * TOP_K + k]),
                                  _row_slab(buf, k * tm + r), sem).start(priority=k % 2)
        return carry

    lax.fori_loop(0, tm, issue, 0)
    xb = xb_ref[0]
    hg = jnp.dot(xb, sg_ref[...], preferred_element_type=F32)
    hu = jnp.dot(xb, su_ref[...], preferred_element_type=F32)
    y_sh = _dot(hg * _sigmoid(hg) * hu, sd_ref[...])
    pltpu.make_async_copy(ys_hbm.at[pl.ds(0, buf.shape[0]), :], buf, sem).wait()
    w = wt_ref[0]
    rows = []
    for j in range(TOK_ROWS):
        acc = w[:, 0:1] * buf[pl.ds(j, tm, stride=TOK_ROWS), :]
        for k in range(1, TOP_K):
            acc = acc + w[:, k:k + 1] * buf[pl.ds(k * tm * TOK_ROWS + j, tm, stride=TOK_ROWS), :]
        rows.append(acc)
    x2 = x1_ref[0] + mod_ref[0, 0, 5:6, :] * (_unpack_rows(rows) + y_sh)
    if final:
        x2 = _rms(x2, fg_ref[...])
    out_ref[0] = x2


def _ffn_out(dest_tiles, x1, xb, wts, ys, mod, sg, su, sd, final_g, ctx_tiles, tile0, final):
    b, s, d = x1.shape
    tm = TOKEN_TILE
    tiles = s // tm
    tok = lambda bi, i: (bi, i, 0)
    full2 = lambda bi, i: (0, 0)
    return pl.pallas_call(
        functools.partial(_ffn_out_kernel, final=final),
        grid=(b, tiles),
        in_specs=[pl.BlockSpec((1, 1, tm * TOP_K), lambda bi, i: (bi * tiles + i, 0, 0),
                               memory_space=pltpu.SMEM),
                  pl.BlockSpec((1, tm, d), tok),
                  pl.BlockSpec((1, tm, d), tok),
                  pl.BlockSpec((1, tm, LANE), tok),
                  pl.BlockSpec((1, 1, 8, d),
                               lambda bi, i: (bi, jnp.minimum((i + tile0) // ctx_tiles, 1), 0, 0)),
                  pl.BlockSpec(sg.shape, full2),
                  pl.BlockSpec(su.shape, full2),
                  pl.BlockSpec(sd.shape, full2),
                  pl.BlockSpec((1, d), full2),
                  pl.BlockSpec(memory_space=pl.ANY)],
        out_specs=pl.BlockSpec((1, tm, d), tok),
        out_shape=jax.ShapeDtypeStruct((b, s, d), F32),
        scratch_shapes=[pltpu.VMEM((TOP_K * tm * TOK_ROWS, LANE), ROW_DTYPE), pltpu.SemaphoreType.DMA(())],
        compiler_params=_cparams("parallel", "arbitrary"),
        name="ffn_out",
    )(dest_tiles, x1, xb, wts, mod, sg, su, sd, final_g, ys)


def _pos_embed_2d(rows, dim):
    quarter = dim // 4
    omega = 1.0 / (10000.0 ** (jnp.arange(quarter, dtype=F32) / quarter))
    row = jnp.repeat(jnp.arange(rows, dtype=F32), GRID_W)
    col = jnp.tile(jnp.arange(GRID_W, dtype=F32), rows)

    def axis_embed(p):
        ang = p[:, None] * omega[None, :]
        return jnp.concatenate([jnp.sin(ang), jnp.cos(ang)], axis=-1)

    return jnp.concatenate([axis_embed(row), axis_embed(col)], axis=-1)


def _pad_heads(w, n_heads, axis):
    shape = w.shape
    hd = shape[axis] // n_heads
    w = w.reshape(shape[:axis] + (n_heads, hd) + shape[axis + 1:])
    pad = [(0, 0)] * w.ndim
    pad[axis + 1] = (0, LANE - hd)
    w = jnp.pad(w, pad)
    return w.reshape(shape[:axis] + (n_heads * LANE,) + shape[axis + 1:])


def _routing_tables(n_assign, counts, bm):
    n_exp = counts.shape[0]
    padded = (counts + bm - 1) // bm * bm
    pad_end = jnp.cumsum(padded)
    pad_start = pad_end - padded
    n_blocks = -(-(n_assign + n_exp * (bm - 1)) // bm)
    block_start = jnp.arange(n_blocks, dtype=jnp.int32) * bm
    block_e = jnp.minimum(jnp.sum(pad_end[None, :] <= block_start[:, None], axis=1), n_exp - 1)
    n_used = (pad_end[-1] // bm).reshape(1)
    has_pad = ((pad_end[block_e] == block_start + bm) & (counts[block_e] % bm != 0)) | (block_start >= pad_end[-1])
    slot = jnp.cumsum(has_pad) - 1
    block_id = jnp.arange(n_blocks, dtype=jnp.int32)
    pad_blocks = jnp.sum(jnp.where(has_pad[None, :] & (slot[None, :] == block_id[:, None]), block_id[None, :], 0),
                         axis=1)
    n_pad_blocks = jnp.sum(has_pad).reshape(1)
    return (jnp.broadcast_to(pad_start.astype(F32)[:, None], (n_exp, TOKEN_TILE)),
            block_e.astype(jnp.int32), n_used.astype(jnp.int32),
            pad_blocks.astype(jnp.int32), n_pad_blocks.astype(jnp.int32), n_blocks * bm)


def kernel(x, c, ctx, c_ctx, norm1_g, norm2_g, w_mod, b_mod, w_in, rg_conv_w, rg_conv_b, rg_gate_w, rg_gate_b, rg_lambda, mlstm_gate_b, mlstm_norm_g, cm_norm_g, cm_w, cm_b, w_out, router_w, router_b, exp_gate, exp_up, exp_down, sh_gate, sh_up, sh_down, final_norm_g):
    batch, seq, d = x.shape
    ctx_len = ctx.shape[1]
    depth = w_in.shape[0]
    rg_w = rg_conv_w.shape[2]
    m_w = mlstm_norm_g.shape[1]
    cm_width = cm_norm_g.shape[1]
    head_dim = m_w // M_HEADS
    rg_hd = rg_w // RG_HEADS
    n_exp = router_w.shape[2]
    n_gates = 4 * M_HEADS
    assert ctx_len % TOKEN_TILE == 0 and seq % TOKEN_TILE == 0
    assert rg_w % LANE == 0 and LANE % rg_hd == 0 and head_dim < LANE and cm_width % LANE == 0
    assert d == TOK_ROWS * LANE
    assert TOP_K <= SUBLANE and N_GROUPS <= SUBLANE and n_exp % (N_GROUPS * SUBLANE) == 0
    ctx_tiles = ctx_len // TOKEN_TILE
    s_all = ctx_len + seq

    xall = jnp.concatenate([ctx, x + _pos_embed_2d(seq // GRID_W, d).astype(x.dtype)[None]], axis=1)
    cond = jnp.concatenate([c, c_ctx[None], jnp.zeros((SUBLANE - (batch + 1) % SUBLANE, d), F32)], axis=0)

    exp_gate_t = jnp.swapaxes(exp_gate, 2, 3)
    exp_up_t = jnp.swapaxes(exp_up, 2, 3)

    out = None
    for l in range(depth):
        last = l == depth - 1
        sizes = (rg_w, rg_w, m_w, m_w, m_w, m_w, n_gates, 2 * cm_width)
        offs = np.cumsum((0,) + sizes)
        w_cols = [w_in[l][:, offs[i]:offs[i + 1]] for i in range(len(sizes))]
        w_gate = w_cols[6].reshape(d, 2, 2, M_HEADS)
        lane_pad = ((0, 0), (0, LANE - 2 * M_HEADS))
        w_p = jnp.concatenate(
            [w_cols[0], w_cols[1]] + [_pad_heads(w_cols[i], M_HEADS, 1) for i in (2, 3, 4, 5)]
            + [jnp.pad(w_gate[:, :, g, :].reshape(d, 2 * M_HEADS), lane_pad) for g in range(2)]
            + [w_cols[7]], axis=1).astype(MXU_DTYPE)
        splits = (2 * rg_w, 4 * M_HEADS * LANE, 2 * LANE, 2 * cm_width)
        hp = LANE // rg_hd
        gw = rg_gate_w[l].reshape(4, RG_HEADS // hp, hp, rg_hd, rg_hd)
        eye = jnp.eye(hp, dtype=F32)
        wg = jnp.einsum('gthij,hk->tghikj', gw, eye).reshape(RG_HEADS // hp, 4, LANE, LANE)
        wg = wg.transpose(0, 2, 1, 3).reshape(RG_HEADS // hp, LANE, 4 * LANE).astype(MXU_DTYPE)
        bg = rg_gate_b[l].reshape(4, RG_HEADS // hp, LANE).transpose(1, 0, 2).reshape(RG_HEADS // hp, 1, 4 * LANE)
        gate_b = jnp.pad(mlstm_gate_b[l].transpose(1, 0, 2).reshape(2, 2 * M_HEADS), lane_pad)
        m_norm_g = _pad_heads(mlstm_norm_g[l], M_HEADS, 0).reshape(M_HEADS, 1, LANE)
        cm_ws = cm_w[l].astype(MXU_DTYPE)
        cm_bmap = jnp.repeat(cm_b[l].T, cm_width // CM_GROUPS, axis=1)
        w_o = w_out[l]
        w_r = w_o[:rg_w].astype(MXU_DTYPE)
        w_m = _pad_heads(w_o[rg_w:rg_w + m_w], M_HEADS, 0).astype(MXU_DTYPE)
        w_c = w_o[rg_w + m_w:].astype(MXU_DTYPE)

        mod = _adaln(cond, w_mod, b_mod, l).reshape(cond.shape[0], 6, d)
        mod = jnp.pad(mod, ((0, 0), (0, 2), (0, 0)))
        mod = jnp.stack([jnp.broadcast_to(mod[batch], (batch, 8, d)), mod[:batch]], axis=1)

        rg, qkvo, gates, cm = _inproj(xall, mod, norm1_g[l].reshape(1, d), w_p, splits, ctx_tiles)
        y_rg = _rglru(rg, rg_conv_w[l], rg_conv_b[l].reshape(1, rg_w), wg, bg, rg_lambda[l], ctx_len)
        y_m = _mlstm(qkvo, gates, gate_b, m_norm_g, ctx_len, head_dim)
        y_cm = _gmlp(cm, cm_norm_g[l].reshape(1, cm_width), cm_ws, cm_bmap)
        tile0 = ctx_tiles if last else 0
        n_tok = batch * (s_all - tile0 * TOKEN_TILE)
        router_bias = jnp.broadcast_to(router_b[l][:, None], (n_exp, TOKEN_TILE))
        x1, xb, xrow, eidx, wts, rank, counts = _outproj_router(
            y_rg, y_m, y_cm, xall, mod, norm2_g[l].reshape(1, d), w_r, w_m, w_c, router_w[l],
            router_bias, ctx_tiles, tile0)

        pad_start, block_e, n_used, pad_blocks, n_pad_blocks, n_rows = _routing_tables(
            n_tok * TOP_K, counts[:, 0], MOE_BLOCK)
        dest = _dest_rows(eidx, rank, pad_start)
        dest_tiles = dest[:, :TOP_K].transpose(0, 2, 1).reshape(n_tok // TOKEN_TILE, 1, TOKEN_TILE * TOP_K)
        wts = jnp.pad(wts[:, :TOP_K].transpose(0, 2, 1), ((0, 0), (0, 0), (0, LANE - TOP_K)))
        xs = _dispatch(pad_blocks, n_pad_blocks, dest_tiles, xrow, n_rows)
        ys = _moe_ffn(block_e, n_used, xs, exp_gate_t, exp_up_t, exp_down, l)
        out = _ffn_out(dest_tiles, x1, xb, wts, ys, mod, sh_gate[l].astype(MXU_DTYPE),
                       sh_up[l].astype(MXU_DTYPE), sh_down[l].astype(MXU_DTYPE),
                       final_norm_g.reshape(1, d), ctx_tiles, tile0, last)
        xall = out
    return out
```
